```python
import math
import jax, jax.numpy as jnp
from jax import lax
import numpy as np

D_MODEL = 1024
BATCH = 1
SEQ = 16384
DEPTH = 4
DEC_BATCH = 2
DEC_SEQ = 8192
PAST_LEN = 128

GRID_W = 64
N_MIXERS = 2
HEAD_DIM = 64
N_HEADS = D_MODEL // HEAD_DIM
N_KV_HEADS = N_HEADS // 4
GQA_GROUP = N_HEADS // N_KV_HEADS
QKV_DIM = (N_HEADS + 2 * N_KV_HEADS) * HEAD_DIM
Q_BLOCK = 128
ROPE_THETA = 10000.0
AXIS_DIM = HEAD_DIM // 2
NA_HEADS = D_MODEL // HEAD_DIM
NA_ROWS_MAX = 8
NA_COLS = 16
N_EXPERTS = 32
TOP_K = 4
D_FF = D_MODEL
SWIGLU_LIMIT = 7.0
SWIGLU_ALPHA = 1.702
MOE_CHUNK = 128
DN_ALPHA = (2 * DEPTH) ** 0.25
DN_BETA = (8 * DEPTH) ** -0.25
N_GQA_LAYERS = (DEPTH + 1) // 2
N_NA_LAYERS = DEPTH // 2
LN_EPS = 1e-5
RMS_EPS = 1e-6

kernel_name = "hybrid_gqa_natten_moe_deepnorm_encoder"


def layer_norm(x, g, b):
    xf = x.astype(jnp.float32)
    mu = jnp.mean(xf, axis=-1, keepdims=True)
    var = jnp.mean(jnp.square(xf - mu), axis=-1, keepdims=True)
    y = (xf - mu) * lax.rsqrt(var + LN_EPS) * g.astype(jnp.float32) + b.astype(jnp.float32)
    return y.astype(x.dtype)


def rms_norm(x, g):
    xf = x.astype(jnp.float32)
    y = xf * lax.rsqrt(jnp.mean(jnp.square(xf), axis=-1, keepdims=True) + RMS_EPS) * g.astype(jnp.float32)
    return y.astype(x.dtype)


def axial_angles(seq_len):
    t = jnp.arange(seq_len)
    row = (t // GRID_W).astype(jnp.float32)
    col = (t % GRID_W).astype(jnp.float32)
    freqs = ROPE_THETA ** (-jnp.arange(0, AXIS_DIM, 2, dtype=jnp.float32) / AXIS_DIM)
    return row[:, None] * freqs, col[:, None] * freqs


def rope_half(x, ang):
    c = jnp.cos(ang)[None, :, None, :].astype(x.dtype)
    s = jnp.sin(ang)[None, :, None, :].astype(x.dtype)
    half = AXIS_DIM // 2
    x1, x2 = x[..., :half], x[..., half:]
    return jnp.concatenate([x1 * c - x2 * s, x2 * c + x1 * s], axis=-1)


def axial_rope(x, ang_row, ang_col):
    return jnp.concatenate([rope_half(x[..., :AXIS_DIM], ang_row),
                            rope_half(x[..., AXIS_DIM:], ang_col)], axis=-1)


def gqa_mixer(x, w_qkv, q_g, k_g, w_o):
    B, S, _ = x.shape
    qkv = x @ w_qkv
    nq_cols = N_HEADS * HEAD_DIM
    nk_cols = N_KV_HEADS * HEAD_DIM
    q = qkv[..., :nq_cols].reshape(B, S, N_HEADS, HEAD_DIM)
    k = qkv[..., nq_cols:nq_cols + nk_cols].reshape(B, S, N_KV_HEADS, HEAD_DIM)
    v = qkv[..., nq_cols + nk_cols:].reshape(B, S, N_KV_HEADS, HEAD_DIM)
    q = rms_norm(q, q_g)
    k = rms_norm(k, k_g)
    ang_r, ang_c = axial_angles(S)
    q = axial_rope(q, ang_r, ang_c) * (HEAD_DIM ** -0.5)
    k = axial_rope(k, ang_r, ang_c)
    n_blk = S // Q_BLOCK
    qb = q.reshape(B, n_blk, Q_BLOCK, N_KV_HEADS, GQA_GROUP, HEAD_DIM).transpose(1, 0, 2, 3, 4, 5)

    def block(q_blk):
        s = jnp.einsum('bqkgd,bskd->bkgqs', q_blk, k).astype(jnp.float32)
        p = jax.nn.softmax(s, axis=-1).astype(v.dtype)
        return jnp.einsum('bkgqs,bskd->bqkgd', p, v)

    o = lax.map(block, qb)
    o = o.transpose(1, 0, 2, 3, 4, 5).reshape(B, S, N_HEADS * HEAD_DIM)
    return o @ w_o


def neighbourhood_mixer(x, w_qkv, rpb, w_o):
    B, S, _ = x.shape
    rows = S // GRID_W
    kr = min(NA_ROWS_MAX, rows)
    qkv = (x @ w_qkv).reshape(B, rows, GRID_W, 3, NA_HEADS, HEAD_DIM)
    q = qkv[..., 0, :, :] * (HEAD_DIM ** -0.5)
    k = qkv[..., 1, :, :]
    v = qkv[..., 2, :, :]
    cols = jnp.arange(GRID_W)
    col_start = jnp.clip(cols - NA_COLS // 2, 0, GRID_W - NA_COLS)
    col_idx = col_start[:, None] + jnp.arange(NA_COLS)
    dc_idx = col_idx - cols[:, None] + (NA_COLS - 1)
    bias_c = rpb[:, :, dc_idx]

    def row_block(r):
        rs = jnp.clip(r - kr // 2, 0, rows - kr)
        q_r = lax.dynamic_index_in_dim(q, r, axis=1, keepdims=False)
        k_n = lax.dynamic_slice_in_dim(k, rs, kr, axis=1)[:, :, col_idx]
        v_n = lax.dynamic_slice_in_dim(v, rs, kr, axis=1)[:, :, col_idx]
        dr_idx = rs + jnp.arange(kr) - r + (NA_ROWS_MAX - 1)
        bias = jnp.take(bias_c, dr_idx, axis=1).transpose(0, 2, 1, 3)
        s = jnp.einsum('bwhd,biwjhd->bhwij', q_r, k_n).astype(jnp.float32) + bias[None].astype(jnp.float32)
        p = jax.nn.softmax(s.reshape(B, NA_HEADS, GRID_W, kr * NA_COLS), axis=-1)
        p = p.reshape(B, NA_HEADS, GRID_W, kr, NA_COLS).astype(v.dtype)
        return jnp.einsum('bhwij,biwjhd->bwhd', p, v_n)

    o = lax.map(row_block, jnp.arange(rows))
    o = jnp.moveaxis(o, 0, 1).reshape(B, S, NA_HEADS * HEAD_DIM)
    return o @ w_o


def moe(x, w_r, b_r, w_gu, b_gu, w_dn, b_dn):
    B, S, D = x.shape
    xt = x.reshape(-1, D)
    T = xt.shape[0]
    logits = (xt @ w_r).astype(jnp.float32) + b_r.astype(jnp.float32)
    vals, idx = lax.top_k(logits, TOP_K)
    wts = jax.nn.softmax(vals, axis=-1)
    gates = jnp.sum(jax.nn.one_hot(idx, N_EXPERTS, dtype=jnp.float32) * wts[..., None], axis=1).astype(x.dtype)
    n_chunk = T // MOE_CHUNK

    def chunk(args):
        xc, gc = args
        h = jnp.einsum('td,edf->tef', xc, w_gu) + b_gu
        g = jnp.minimum(h[..., 0::2], SWIGLU_LIMIT)
        u = jnp.clip(h[..., 1::2], -SWIGLU_LIMIT, SWIGLU_LIMIT)
        a = (u + 1.0) * (g * jax.nn.sigmoid(SWIGLU_ALPHA * g)) * gc[..., None]
        return jnp.einsum('tef,efd->td', a, w_dn) + gc @ b_dn

    out = lax.map(chunk, (xt.reshape(n_chunk, MOE_CHUNK, D), gates.reshape(n_chunk, MOE_CHUNK, N_EXPERTS)))
    return out.reshape(B, S, D)


def trunk(x, gqa_w_qkv, gqa_q_norm, gqa_k_norm, gqa_w_o, na_w_qkv, na_rpb, na_w_o,
          ln1_g, ln1_b, ln2_g, ln2_b, router_w, router_b, exp_w_gu, exp_b_gu, exp_w_down, exp_b_down):
    for i in range(DEPTH):
        j = i // N_MIXERS
        if i % N_MIXERS == 0:
            h = gqa_mixer(x, gqa_w_qkv[j], gqa_q_norm[j], gqa_k_norm[j], gqa_w_o[j])
        else:
            h = neighbourhood_mixer(x, na_w_qkv[j], na_rpb[j], na_w_o[j])
        x = layer_norm(DN_ALPHA * x + h, ln1_g[i], ln1_b[i])
        f = moe(x, router_w[i], router_b[i], exp_w_gu[i], exp_b_gu[i], exp_w_down[i], exp_b_down[i])
        x = layer_norm(DN_ALPHA * x + f, ln2_g[i], ln2_b[i])
    return x


def setup_inputs(seed: int = 0) -> dict:
    key = jax.random.key(seed)
    ks = jax.random.split(key, 20)
    f32 = jnp.float32
    s_in = D_MODEL ** -0.5
    gqa_col_scale = jnp.concatenate([jnp.ones(((N_HEADS + N_KV_HEADS) * HEAD_DIM,), f32),
                                     jnp.full((N_KV_HEADS * HEAD_DIM,), DN_BETA, f32)])
    na_col_scale = jnp.concatenate([jnp.ones((2 * D_MODEL,), f32), jnp.full((D_MODEL,), DN_BETA, f32)])
    return {
        "x_prompt": jax.random.normal(ks[0], (BATCH, SEQ, D_MODEL), f32),
        "x_sample": jax.random.normal(ks[1], (DEC_BATCH, DEC_SEQ, D_MODEL), f32),
        "gqa_w_qkv": jax.random.normal(ks[2], (N_GQA_LAYERS, D_MODEL, QKV_DIM), f32) * s_in * gqa_col_scale,
        "gqa_q_norm": 1.0 + 0.02 * jax.random.normal(ks[3], (N_GQA_LAYERS, HEAD_DIM), f32),
        "gqa_k_norm": 1.0 + 0.02 * jax.random.normal(ks[4], (N_GQA_LAYERS, HEAD_DIM), f32),
        "gqa_w_o": jax.random.normal(ks[5], (N_GQA_LAYERS, D_MODEL, D_MODEL), f32) * s_in * DN_BETA,
        "na_w_qkv": jax.random.normal(ks[6], (N_NA_LAYERS, D_MODEL, 3 * D_MODEL), f32) * s_in * na_col_scale,
        "na_rpb": 0.02 * jax.random.normal(ks[7], (N_NA_LAYERS, NA_HEADS, 2 * NA_ROWS_MAX - 1, 2 * NA_COLS - 1), f32),
        "na_w_o": jax.random.normal(ks[8], (N_NA_LAYERS, D_MODEL, D_MODEL), f32) * s_in * DN_BETA,
        "ln1_g": 1.0 + 0.02 * jax.random.normal(ks[9], (DEPTH, D_MODEL), f32),
        "ln1_b": 0.02 * jax.random.normal(ks[10], (DEPTH, D_MODEL), f32),
        "ln2_g": 1.0 + 0.02 * jax.random.normal(ks[11], (DEPTH, D_MODEL), f32),
        "ln2_b": 0.02 * jax.random.normal(ks[12], (DEPTH, D_MODEL), f32),
        "router_w": jax.random.normal(ks[13], (DEPTH, D_MODEL, N_EXPERTS), f32) * s_in,
        "router_b": 0.01 * jax.random.normal(ks[14], (DEPTH, N_EXPERTS), f32),
        "exp_w_gu": jax.random.normal(ks[15], (DEPTH, N_EXPERTS, D_MODEL, 2 * D_FF), f32) * s_in,
        "exp_b_gu": 0.01 * jax.random.normal(ks[16], (DEPTH, N_EXPERTS, 2 * D_FF), f32),
        "exp_w_down": jax.random.normal(ks[17], (DEPTH, N_EXPERTS, D_FF, D_MODEL), f32) * (D_FF ** -0.5) * DN_BETA,
        "exp_b_down": 0.01 * jax.random.normal(ks[18], (DEPTH, N_EXPERTS, D_MODEL), f32),
    }


def reference(x_prompt, x_sample, gqa_w_qkv, gqa_q_norm, gqa_k_norm, gqa_w_o, na_w_qkv, na_rpb, na_w_o,
              ln1_g, ln1_b, ln2_g, ln2_b, router_w, router_b, exp_w_gu, exp_b_gu, exp_w_down, exp_b_down):
    y_prompt = trunk(x_prompt, gqa_w_qkv, gqa_q_norm, gqa_k_norm, gqa_w_o, na_w_qkv, na_rpb, na_w_o,
                     ln1_g, ln1_b, ln2_g, ln2_b, router_w, router_b, exp_w_gu, exp_b_gu, exp_w_down, exp_b_down)
    y_sample = trunk(x_sample, gqa_w_qkv, gqa_q_norm, gqa_k_norm, gqa_w_o, na_w_qkv, na_rpb, na_w_o,
                     ln1_g, ln1_b, ln2_g, ln2_b, router_w, router_b, exp_w_gu, exp_b_gu, exp_w_down, exp_b_down)
    return (y_prompt, y_sample)
```

```python
import functools
import math

import numpy as np
import jax
import jax.numpy as jnp
from jax import lax
from jax.experimental import pallas as pl
from jax.experimental.pallas import tpu as pltpu

F32 = jnp.float32
BF16 = jnp.bfloat16
I32 = jnp.int32

D_MODEL = 1024
DEPTH = 4
GRID_W = 64
HEAD_DIM = 64
N_HEADS = 16
N_KV_HEADS = 4
GQA_GROUP = 4
QKV_DIM = 1536
ROPE_THETA = 10000.0
AXIS_DIM = 32
NA_ROWS = 8
NA_COLS = 16
N_EXPERTS = 32
TOP_K = 4
D_FF = 1024
SWIGLU_LIMIT = 7.0
SWIGLU_ALPHA = 1.702
DN_ALPHA = (2 * DEPTH) ** 0.25
LN_EPS = 1e-5
RMS_EPS = 1e-6
LOG2E = 1.4426950408889634
NEG_BIG = -1e30

VMEM_LIMIT = 56 * 1024 * 1024
TOK_TILE = 512
Q_TILE = 256
KV_CHUNK = 256
NA_QROWS = 8
NA_KROWS = 16
EXP_TILE = 512

NT_DIMS = (((1,), (1,)), ((), ()))
TN_DIMS = (((0,), (0,)), ((), ()))


def _cparams(sem):
    return pltpu.CompilerParams(dimension_semantics=sem, vmem_limit_bytes=VMEM_LIMIT)


def _layer_norm(y, g, b):
    mu = jnp.mean(y, axis=-1, keepdims=True)
    yc = y - mu
    var = jnp.mean(yc * yc, axis=-1, keepdims=True)
    return yc * lax.rsqrt(var + LN_EPS) * g + b


def _rope_t(y, tab):
    cr, sr, cc, sc = tab[0:16], tab[16:32], tab[32:48], tab[48:64]
    a, b, c, d = y[0:16], y[16:32], y[32:48], y[48:64]
    return jnp.concatenate([a * cr - b * sr, b * cr + a * sr, c * cc - d * sc, d * cc + c * sc], axis=0)


def _gqa_prep_kernel(x_ref, w_ref, qg_ref, kg_ref, tab_ref, qT_ref, k_ref, vT_ref):
    xb = x_ref[...].astype(BF16)
    accT = lax.dot_general(w_ref[...], xb, NT_DIMS, preferred_element_type=F32)
    tab = tab_ref[...]

    def normrope(blk, g):
        ms = jnp.mean(blk * blk, axis=0, keepdims=True)
        return _rope_t(blk * lax.rsqrt(ms + RMS_EPS) * g, tab)

    qg = qg_ref[...]
    kg = kg_ref[...]
    for h in range(N_HEADS):
        y = normrope(accT[HEAD_DIM * h:HEAD_DIM * (h + 1)], qg) * (HEAD_DIM ** -0.5 * LOG2E)
        qT_ref[HEAD_DIM * h:HEAD_DIM * (h + 1), :] = y.astype(BF16)
    k0 = N_HEADS * HEAD_DIM
    ks = [normrope(accT[k0 + HEAD_DIM * h:k0 + HEAD_DIM * (h + 1)], kg) for h in range(N_KV_HEADS)]
    k_ref[...] = jnp.concatenate(ks, axis=0).T.astype(BF16)
    v0 = k0 + N_KV_HEADS * HEAD_DIM
    vT_ref[...] = accT[v0:v0 + N_KV_HEADS * HEAD_DIM].astype(BF16)


def _gqa_prep(x, wT, qg, kg, tab):
    T = x.shape[0]
    t = TOK_TILE
    kvd = N_KV_HEADS * HEAD_DIM
    return pl.pallas_call(
        _gqa_prep_kernel,
        grid=(T // t,),
        in_specs=[
            pl.BlockSpec((t, D_MODEL), lambda i: (i, 0)),
            pl.BlockSpec((QKV_DIM, D_MODEL), lambda i: (0, 0)),
            pl.BlockSpec((HEAD_DIM, 1), lambda i: (0, 0)),
            pl.BlockSpec((HEAD_DIM, 1), lambda i: (0, 0)),
            pl.BlockSpec((HEAD_DIM, t), lambda i: (0, i)),
        ],
        out_specs=[
            pl.BlockSpec((D_MODEL, t), lambda i: (0, i)),
            pl.BlockSpec((t, kvd), lambda i: (i, 0)),
            pl.BlockSpec((kvd, t), lambda i: (0, i)),
        ],
        out_shape=[
            jax.ShapeDtypeStruct((D_MODEL, T), BF16),
            jax.ShapeDtypeStruct((T, kvd), BF16),
            jax.ShapeDtypeStruct((kvd, T), BF16),
        ],
        compiler_params=_cparams(("parallel",)),
        name="gqa_prep",
    )(x, wT, qg, kg, tab)


def _gqa_attn_kernel(kvs_ref, kvn_ref, q_ref, k_ref, vT_ref, o_ref):
    kh = pl.program_id(0)
    qi = pl.program_id(1)
    tq = q_ref.shape[1]
    M = GQA_GROUP * tq
    q4 = q_ref[...]
    par = kh % 2
    zero = jnp.zeros((HEAD_DIM, tq), BF16)
    cols = []
    for g in range(GQA_GROUP):
        qg = q4[HEAD_DIM * g:HEAD_DIM * (g + 1)]
        cols.append(jnp.concatenate([jnp.where(par == 0, qg, zero), jnp.where(par == 1, qg, zero)], axis=0))
    qcat = jnp.concatenate(cols, axis=1)
    start = kvs_ref[qi]
    nchunk = kvn_ref[qi]

    def body(j, carry):
        m, l, acc = carry
        off = pl.multiple_of(start + j * KV_CHUNK, KV_CHUNK)
        kc = k_ref[pl.ds(off, KV_CHUNK), :]
        sT = jnp.dot(kc, qcat, preferred_element_type=F32)
        m_new = jnp.maximum(m, jnp.max(sT, axis=0, keepdims=True))
        alpha = jnp.exp2(m - m_new)
        p = jnp.exp2(sT - m_new)
        l = alpha * l + jnp.sum(p, axis=0, keepdims=True)
        vc = vT_ref[:, pl.ds(off, KV_CHUNK)]
        acc = alpha * acc + jnp.dot(vc, p.astype(BF16), preferred_element_type=F32)
        return m_new, l, acc

    m0 = jnp.full((1, M), -jnp.inf, F32)
    l0 = jnp.zeros((1, M), F32)
    a0 = jnp.zeros((HEAD_DIM, M), F32)
    _, l, acc = lax.fori_loop(0, nchunk, body, (m0, l0, a0))
    oT = acc * (1.0 / l)
    o_ref[...] = jnp.concatenate([oT[:, g * tq:(g + 1) * tq] for g in range(GQA_GROUP)], axis=0).astype(BF16)


def _gqa_attn(qT, k, vT, kv_start, kv_chunks):
    T = qT.shape[1]
    tq = Q_TILE
    grid_spec = pltpu.PrefetchScalarGridSpec(
        num_scalar_prefetch=2,
        grid=(N_KV_HEADS, T // tq),
        in_specs=[
            pl.BlockSpec((GQA_GROUP * HEAD_DIM, tq), lambda kh, qi, a, b: (kh, qi)),
            pl.BlockSpec((T, 2 * HEAD_DIM), lambda kh, qi, a, b: (0, kh // 2)),
            pl.BlockSpec((HEAD_DIM, T), lambda kh, qi, a, b: (kh, 0)),
        ],
        out_specs=pl.BlockSpec((GQA_GROUP * HEAD_DIM, tq), lambda kh, qi, a, b: (kh, qi)),
    )
    return pl.pallas_call(
        _gqa_attn_kernel,
        grid_spec=grid_spec,
        out_shape=jax.ShapeDtypeStruct((D_MODEL, T), BF16),
        compiler_params=_cparams(("parallel", "parallel")),
        name="gqa_attn",
    )(kv_start, kv_chunks, qT, k, vT)


def _na_prep_kernel(x_ref, wqT_ref, wk_ref, wvT_ref, qT_ref, k_ref, vT_ref):
    xb = x_ref[...].astype(BF16)
    qT = lax.dot_general(wqT_ref[...], xb, NT_DIMS, preferred_element_type=F32)
    qT_ref[...] = (qT * (HEAD_DIM ** -0.5 * LOG2E)).astype(BF16)
    k_ref[...] = jnp.dot(xb, wk_ref[...], preferred_element_type=F32).astype(BF16)
    vT_ref[...] = lax.dot_general(wvT_ref[...], xb, NT_DIMS, preferred_element_type=F32).astype(BF16)


def _na_prep(x, wqT, wk, wvT):
    T = x.shape[0]
    t = TOK_TILE
    wspec = pl.BlockSpec((D_MODEL, D_MODEL), lambda i: (0, 0))
    return pl.pallas_call(
        _na_prep_kernel,
        grid=(T // t,),
        in_specs=[pl.BlockSpec((t, D_MODEL), lambda i: (i, 0)), wspec, wspec, wspec],
        out_specs=[
            pl.BlockSpec((D_MODEL, t), lambda i: (0, i)),
            pl.BlockSpec((t, D_MODEL), lambda i: (i, 0)),
            pl.BlockSpec((D_MODEL, t), lambda i: (0, i)),
        ],
        out_shape=[
            jax.ShapeDtypeStruct((D_MODEL, T), BF16),
            jax.ShapeDtypeStruct((T, D_MODEL), BF16),
            jax.ShapeDtypeStruct((D_MODEL, T), BF16),
        ],
        compiler_params=_cparams(("parallel",)),
        name="na_prep",
    )(x, wqT, wk, wvT)


def _na_attn_kernel(koff_ref, case_ref, q_ref, k_ref, vT_ref, bias_ref, o_ref):
    rb = pl.program_id(1)
    nq = q_ref.shape[1]
    nk = NA_KROWS * GRID_W
    off = pl.multiple_of(koff_ref[rb], 2 * HEAD_DIM)
    kc = k_ref[pl.ds(off, nk), :]
    q2 = q_ref[...]
    zero = jnp.zeros((HEAD_DIM, nq), BF16)
    for hh in range(2):
        qh = q2[HEAD_DIM * hh:HEAD_DIM * (hh + 1)]
        qpad = jnp.concatenate([qh, zero] if hh == 0 else [zero, qh], axis=0)
        sT = jnp.dot(kc, qpad, preferred_element_type=F32) + bias_ref[0, hh]
        m = jnp.max(sT, axis=0, keepdims=True)
        p = jnp.exp2(sT - m)
        l = jnp.sum(p, axis=0, keepdims=True)
        vc = vT_ref[HEAD_DIM * hh:HEAD_DIM * (hh + 1), pl.ds(off, nk)]
        oT = jnp.dot(vc, p.astype(BF16), preferred_element_type=F32) * (1.0 / l)
        o_ref[HEAD_DIM * hh:HEAD_DIM * (hh + 1), :] = oT.astype(BF16)


def _na_attn(qT, k, vT, bias, koff, case):
    T = qT.shape[1]
    nq = NA_QROWS * GRID_W
    nk = NA_KROWS * GRID_W
    grid_spec = pltpu.PrefetchScalarGridSpec(
        num_scalar_prefetch=2,
        grid=(N_HEADS // 2, T // nq),
        in_specs=[
            pl.BlockSpec((2 * HEAD_DIM, nq), lambda hp, rb, a, b: (hp, rb)),
            pl.BlockSpec((T, 2 * HEAD_DIM), lambda hp, rb, a, b: (0, hp)),
            pl.BlockSpec((2 * HEAD_DIM, T), lambda hp, rb, a, b: (hp, 0)),
            pl.BlockSpec((1, 2, nk, nq), lambda hp, rb, a, b: (b[rb], hp, 0, 0)),
        ],
        out_specs=pl.BlockSpec((2 * HEAD_DIM, nq), lambda hp, rb, a, b: (hp, rb)),
    )
    return pl.pallas_call(
        _na_attn_kernel,
        grid_spec=grid_spec,
        out_shape=jax.ShapeDtypeStruct((D_MODEL, T), BF16),
        compiler_params=_cparams(("parallel", "parallel")),
        name="na_attn",
    )(koff, case, qT, k, vT, bias)


def _na_bias_tables(rpb):
    R = 64
    a = np.arange(NA_QROWS)
    b = np.arange(NA_KROWS)
    row_oh = np.zeros((3, 2 * NA_ROWS - 1, NA_KROWS, NA_QROWS), np.float32)
    row_ok = np.zeros((3, NA_KROWS, NA_QROWS), bool)
    for c, r0 in enumerate((0, 24, R - NA_QROWS)):
        ws = min(max(r0 - NA_ROWS // 2, 0), R - NA_KROWS)
        r = r0 + a
        rs = np.clip(r - NA_ROWS // 2, 0, R - NA_ROWS)
        krow = ws + b
        rel = krow[:, None] - rs[None, :]
        ok = (rel >= 0) & (rel < NA_ROWS)
        dr = krow[:, None] - r[None, :] + (NA_ROWS - 1)
        row_ok[c] = ok
        for bi in range(NA_KROWS):
            for ai in range(NA_QROWS):
                if ok[bi, ai]:
                    row_oh[c, dr[bi, ai], bi, ai] = 1.0
    w = np.arange(GRID_W)
    cs = np.clip(w - NA_COLS // 2, 0, GRID_W - NA_COLS)
    kc = np.arange(GRID_W)
    relc = kc[:, None] - cs[None, :]
    col_ok = (relc >= 0) & (relc < NA_COLS)
    dc = kc[:, None] - w[None, :] + (NA_COLS - 1)
    col_oh = np.zeros((2 * NA_COLS - 1, GRID_W, GRID_W), np.float32)
    for ki in range(GRID_W):
        for wi in range(GRID_W):
            if col_ok[ki, wi]:
                col_oh[dc[ki, wi], ki, wi] = 1.0
    ok = row_ok[:, :, None, :, None] & col_ok[None, None, :, None, :]
    t = jnp.einsum("hrc,xrba,ckw->xhbkaw", rpb * LOG2E, jnp.asarray(row_oh), jnp.asarray(col_oh),
                   precision=lax.Precision.HIGHEST)
    t = jnp.where(jnp.asarray(ok)[:, None], t, NEG_BIG)
    return t.reshape(3, N_HEADS, NA_KROWS * GRID_W, NA_QROWS * GRID_W)


def _proj_ln_router_kernel(oT_ref, wo_ref, x_ref, g_ref, b_ref, wrT_ref, br_ref,
                           x1_ref, eid_ref, wts_ref, rank_ref, cnt_ref, carry_ref):
    i = pl.program_id(0)
    t = x_ref.shape[0]

    @pl.when(i == 0)
    def _():
        carry_ref[...] = jnp.zeros_like(carry_ref)

    h = lax.dot_general(oT_ref[...], wo_ref[...], TN_DIMS, preferred_element_type=F32)
    x1 = _layer_norm(DN_ALPHA * x_ref[...] + h, g_ref[...], b_ref[...])
    x1_ref[...] = x1

    xh = x1.astype(BF16)
    xl = (x1 - xh.astype(F32)).astype(BF16)
    wr = wrT_ref[...]
    wh = wr.astype(BF16)
    wl = (wr - wh.astype(F32)).astype(BF16)
    logits = (lax.dot_general(wh, xh, NT_DIMS, preferred_element_type=F32)
              + lax.dot_general(wh, xl, NT_DIMS, preferred_element_type=F32)
              + lax.dot_general(wl, xh, NT_DIMS, preferred_element_type=F32)) + br_ref[...]

    eidx = lax.broadcasted_iota(I32, (N_EXPERTS, t), 0).astype(F32)
    work = logits
    vals, ids, ohs = [], [], []
    for _ in range(TOP_K):
        mx = jnp.max(work, axis=0, keepdims=True)
        idx = jnp.min(jnp.where(work == mx, eidx, float(N_EXPERTS)), axis=0, keepdims=True)
        oh = eidx == idx
        vals.append(mx)
        ids.append(idx)
        ohs.append(oh)
        work = jnp.where(oh, -jnp.inf, work)
    es = [jnp.exp(v - vals[0]) for v in vals]
    inv = 1.0 / (es[0] + es[1] + es[2] + es[3])
    zrow = jnp.zeros((8 - TOP_K, t), F32)
    wts_ref[...] = jnp.concatenate([e * inv for e in es] + [zrow], axis=0)
    eid_ref[...] = jnp.concatenate(ids + [zrow], axis=0).astype(I32)

    oh_all = jnp.zeros((N_EXPERTS, t), F32)
    for oh in ohs:
        oh_all = oh_all + oh.astype(F32)
    tri = (lax.broadcasted_iota(I32, (t, t), 0) < lax.broadcasted_iota(I32, (t, t), 1)).astype(BF16)
    base = carry_ref[...] + jnp.dot(oh_all.astype(BF16), tri, preferred_element_type=F32)
    ranks = [jnp.sum(jnp.where(oh, base, 0.0), axis=0, keepdims=True) for oh in ohs]
    rank_ref[...] = jnp.concatenate(ranks + [zrow], axis=0).astype(I32)
    carry = carry_ref[...] + jnp.sum(oh_all, axis=1, keepdims=True)
    carry_ref[...] = carry
    cnt_ref[...] = jnp.broadcast_to(carry, cnt_ref.shape).astype(I32)


def _proj_ln_router(oT, wo, x, g, b, wrT, br):
    T = x.shape[0]
    t = TOK_TILE
    row = lambda i: (i, 0)
    col = lambda i: (0, i)
    fixed = lambda i: (0, 0)
    return pl.pallas_call(
        _proj_ln_router_kernel,
        grid=(T // t,),
        in_specs=[
            pl.BlockSpec((D_MODEL, t), col),
            pl.BlockSpec((D_MODEL, D_MODEL), fixed),
            pl.BlockSpec((t, D_MODEL), row),
            pl.BlockSpec((1, D_MODEL), fixed),
            pl.BlockSpec((1, D_MODEL), fixed),
            pl.BlockSpec((N_EXPERTS, D_MODEL), fixed),
            pl.BlockSpec((N_EXPERTS, 1), fixed),
        ],
        out_specs=[
            pl.BlockSpec((t, D_MODEL), row),
            pl.BlockSpec((8, t), col),
            pl.BlockSpec((8, t), col),
            pl.BlockSpec((8, t), col),
            pl.BlockSpec((N_EXPERTS, 128), fixed),
        ],
        out_shape=[
            jax.ShapeDtypeStruct((T, D_MODEL), F32),
            jax.ShapeDtypeStruct((8, T), I32),
            jax.ShapeDtypeStruct((8, T), F32),
            jax.ShapeDtypeStruct((8, T), I32),
            jax.ShapeDtypeStruct((N_EXPERTS, 128), I32),
        ],
        scratch_shapes=[pltpu.VMEM((N_EXPERTS, 1), F32)],
        compiler_params=_cparams(("arbitrary",)),
        name="proj_ln_router",
    )(oT, wo, x, g, b, wrT, br)


def _experts_kernel(te_ref, nused_ref, xs_ref, gate_ref, wg_ref, wu_ref, bg_ref, bu_ref, wd_ref, bd_ref, y_ref):
    i = pl.program_id(0)

    @pl.when(i < nused_ref[0])
    def _():
        x = xs_ref[...]
        gate = gate_ref[...]
        g = jnp.dot(x, wg_ref[0], preferred_element_type=F32) + bg_ref[0]
        u = jnp.dot(x, wu_ref[0], preferred_element_type=F32) + bu_ref[0]
        g = jnp.minimum(g, SWIGLU_LIMIT)
        u = jnp.clip(u, -SWIGLU_LIMIT, SWIGLU_LIMIT)
        a = (u + 1.0) * (g * jax.nn.sigmoid(SWIGLU_ALPHA * g)) * gate
        y = jnp.dot(a.astype(BF16), wd_ref[0], preferred_element_type=F32) + gate * bd_ref[0]
        y_ref[...] = y.astype(BF16)

    @pl.when(i >= nused_ref[0])
    def _():
        y_ref[...] = jnp.zeros_like(y_ref)


def _experts(tile_expert, n_used, xs, gate, wg, wu, bg, bu, wd, bd):
    P = xs.shape[0]
    r = EXP_TILE
    wmap = lambda i, te, nu: (te[i], 0, 0)
    grid_spec = pltpu.PrefetchScalarGridSpec(
        num_scalar_prefetch=2,
        grid=(P // r,),
        in_specs=[
            pl.BlockSpec((r, D_MODEL), lambda i, te, nu: (i, 0)),
            pl.BlockSpec((r, 1), lambda i, te, nu: (i, 0)),
            pl.BlockSpec((1, D_MODEL, D_FF), wmap),
            pl.BlockSpec((1, D_MODEL, D_FF), wmap),
            pl.BlockSpec((1, 1, D_FF), wmap),
            pl.BlockSpec((1, 1, D_FF), wmap),
            pl.BlockSpec((1, D_FF, D_MODEL), wmap),
            pl.BlockSpec((1, 1, D_MODEL), wmap),
        ],
        out_specs=pl.BlockSpec((r, D_MODEL), lambda i, te, nu: (i, 0)),
    )
    return pl.pallas_call(
        _experts_kernel,
        grid_spec=grid_spec,
        out_shape=jax.ShapeDtypeStruct((P, D_MODEL), BF16),
        compiler_params=_cparams(("parallel",)),
        name="experts",
    )(tile_expert, n_used, xs, gate, wg, wu, bg, bu, wd, bd)


def _combine_ln_kernel(yg_ref, x_ref, g_ref, b_ref, o_ref):
    f = yg_ref[0].astype(F32)
    for r in range(1, TOP_K):
        f = f + yg_ref[r].astype(F32)
    o_ref[...] = _layer_norm(DN_ALPHA * x_ref[...] + f, g_ref[...], b_ref[...])


def _combine_ln(yg, x, g, b):
    T = x.shape[0]
    t = TOK_TILE
    return pl.pallas_call(
        _combine_ln_kernel,
        grid=(T // t,),
        in_specs=[
            pl.BlockSpec((TOP_K, t, D_MODEL), lambda i: (0, i, 0)),
            pl.BlockSpec((t, D_MODEL), lambda i: (i, 0)),
            pl.BlockSpec((1, D_MODEL), lambda i: (0, 0)),
            pl.BlockSpec((1, D_MODEL), lambda i: (0, 0)),
        ],
        out_specs=pl.BlockSpec((t, D_MODEL), lambda i: (i, 0)),
        out_shape=jax.ShapeDtypeStruct((T, D_MODEL), F32),
        compiler_params=_cparams(("parallel",)),
        name="combine_ln",
    )(yg, x, g, b)


def _moe(x1, eid, wts, rank, counts, wg, wu, bg, bu, wd, bd):
    T = x1.shape[0]
    r = EXP_TILE
    P = T * TOP_K + N_EXPERTS * r
    n_tiles = P // r
    eid = eid[:TOP_K]
    padded = ((counts + (r - 1)) // r) * r
    ends = jnp.cumsum(padded)
    starts = ends - padded
    pos = starts[eid] + rank[:TOP_K]
    tile_expert = jnp.minimum(
        jnp.searchsorted(ends, jnp.arange(n_tiles, dtype=I32) * r, side="right"), N_EXPERTS - 1).astype(I32)
    n_used = (ends[-1] // r).astype(I32).reshape(1)
    flat = pos.reshape(-1)
    tok = jnp.tile(jnp.arange(T, dtype=I32), TOP_K)
    tok_of_row = jnp.zeros((P,), I32).at[flat].set(tok)
    gate_of_row = jnp.zeros((P,), F32).at[flat].set(wts[:TOP_K].reshape(-1))
    xs = jnp.take(x1.astype(BF16), tok_of_row, axis=0)
    y = _experts(tile_expert, n_used, xs, gate_of_row[:, None], wg, wu, bg, bu, wd, bd)
    return jnp.take(y, pos, axis=0)


def _rope_table(seqs, T):
    pos = np.zeros((T,), np.int32)
    for s0, n in seqs:
        pos[s0:s0 + n] = np.arange(n)
    pos = jnp.asarray(pos)
    row = (pos // GRID_W).astype(F32)
    col = (pos % GRID_W).astype(F32)
    freqs = ROPE_THETA ** (-jnp.arange(0, AXIS_DIM, 2, dtype=F32) / AXIS_DIM)
    ar = freqs[:, None] * row[None, :]
    ac = freqs[:, None] * col[None, :]
    return jnp.concatenate([jnp.cos(ar), jnp.sin(ar), jnp.cos(ac), jnp.sin(ac)], axis=0)


def _trunk(x, seqs, gqa_w_qkv, gqa_q_norm, gqa_k_norm, gqa_w_o, na_w_qkv, na_rpb, na_w_o,
           ln1_g, ln1_b, ln2_g, ln2_b, router_w, router_b, exp_w_gu, exp_b_gu, exp_w_down, exp_b_down):
    T = x.shape[0]
    tab = _rope_table(seqs, T)
    kv_start = np.zeros((T // Q_TILE,), np.int32)
    kv_chunks = np.zeros((T // Q_TILE,), np.int32)
    nq = NA_QROWS * GRID_W
    na_koff = np.zeros((T // nq,), np.int32)
    na_case = np.ones((T // nq,), np.int32)
    for s0, n in seqs:
        kv_start[s0 // Q_TILE:(s0 + n) // Q_TILE] = s0
        kv_chunks[s0 // Q_TILE:(s0 + n) // Q_TILE] = n // KV_CHUNK
        rows = n // GRID_W
        for r0 in range(0, rows, NA_QROWS):
            ws = min(max(r0 - NA_ROWS // 2, 0), rows - NA_KROWS)
            rb = (s0 + r0 * GRID_W) // nq
            na_koff[rb] = s0 + ws * GRID_W
            na_case[rb] = 0 if r0 == 0 else (2 if r0 == rows - NA_QROWS else 1)
    kv_start, kv_chunks = jnp.asarray(kv_start), jnp.asarray(kv_chunks)
    na_koff, na_case = jnp.asarray(na_koff), jnp.asarray(na_case)

    for i in range(DEPTH):
        j = i // 2
        if i % 2 == 0:
            wT = gqa_w_qkv[j].T.astype(BF16)
            qT, k, vT = _gqa_prep(x, wT, gqa_q_norm[j][:, None], gqa_k_norm[j][:, None], tab)
            oT = _gqa_attn(qT, k, vT, kv_start, kv_chunks)
            wo = gqa_w_o[j].astype(BF16)
        else:
            w = na_w_qkv[j]
            qT, k, vT = _na_prep(x, w[:, :D_MODEL].T.astype(BF16), w[:, D_MODEL:2 * D_MODEL].astype(BF16),
                                 w[:, 2 * D_MODEL:].T.astype(BF16))
            oT = _na_attn(qT, k, vT, _na_bias_tables(na_rpb[j]), na_koff, na_case)
            wo = na_w_o[j].astype(BF16)
        x1, eid, wts, rank, cnt = _proj_ln_router(oT, wo, x, ln1_g[i][None], ln1_b[i][None],
                                                  router_w[i].T, router_b[i][:, None])
        wgu = exp_w_gu[i]
        bgu = exp_b_gu[i]
        yg = _moe(x1, eid, wts, rank, cnt[:, 0],
                  wgu[..., 0::2].astype(BF16), wgu[..., 1::2].astype(BF16),
                  bgu[:, None, 0::2], bgu[:, None, 1::2],
                  exp_w_down[i].astype(BF16), exp_b_down[i][:, None, :])
        x = _combine_ln(yg, x1, ln2_g[i][None], ln2_b[i][None])
    return x


def kernel(x_prompt, x_sample, gqa_w_qkv, gqa_q_norm, gqa_k_norm, gqa_w_o, na_w_qkv, na_rpb, na_w_o, ln1_g, ln1_b, ln2_g, ln2_b, router_w, router_b, exp_w_gu, exp_b_gu, exp_w_down, exp_b_down):
    bp, sp, d = x_prompt.shape
    bs, ss, _ = x_sample.shape
    x = jnp.concatenate([x_prompt.reshape(bp * sp, d), x_sample.reshape(bs * ss, d)], axis=0)
    seqs = tuple((b * sp, sp) for b in range(bp)) + tuple((bp * sp + b * ss, ss) for b in range(bs))
    y = _trunk(x, seqs, gqa_w_qkv, gqa_q_norm, gqa_k_norm, gqa_w_o, na_w_qkv, na_rpb, na_w_o,
               ln1_g, ln1_b, ln2_g, ln2_b, router_w, router_b, exp_w_gu, exp_b_gu, exp_w_down, exp_b_down)
    return (y[:bp * sp].reshape(bp, sp, d), y[bp * sp:].reshape(bs, ss, d))
```

```python
import functools
import math

import numpy as np
import jax
import jax.numpy as jnp
from jax import lax
from jax.experimental import pallas as pl
from jax.experimental.pallas import tpu as pltpu

F32 = jnp.float32
BF16 = jnp.bfloat16
I32 = jnp.int32

D_MODEL = 1024
DEPTH = 4
GRID_W = 64
HEAD_DIM = 64
N_HEADS = 16
N_KV_HEADS = 4
GQA_GROUP = 4
QKV_DIM = 1536
ROPE_THETA = 10000.0
AXIS_DIM = 32
NA_ROWS = 8
NA_COLS = 16
N_EXPERTS = 32
TOP_K = 4
D_FF = 1024
SWIGLU_LIMIT = 7.0
SWIGLU_ALPHA = 1.702
DN_ALPHA = (2 * DEPTH) ** 0.25
LN_EPS = 1e-5
RMS_EPS = 1e-6
LOG2E = 1.4426950408889634
NEG_BIG = -1e30

VMEM_LIMIT = 56 * 1024 * 1024
TOK_TILE = 512
Q_TILE = 256
KV_CHUNK = 256
NA_QROWS = 8
NA_KROWS = 16
EXP_TILE = 512

NT_DIMS = (((1,), (1,)), ((), ()))
TN_DIMS = (((0,), (0,)), ((), ()))


def _cparams(sem):
    return pltpu.CompilerParams(dimension_semantics=sem, vmem_limit_bytes=VMEM_LIMIT)


def _layer_norm(y, g, b):
    mu = jnp.mean(y, axis=-1, keepdims=True)
    yc = y - mu
    var = jnp.mean(yc * yc, axis=-1, keepdims=True)
    return yc * lax.rsqrt(var + LN_EPS) * g + b


def _rope_t(y, tab):
    cr, sr, cc, sc = tab[0:16], tab[16:32], tab[32:48], tab[48:64]
    a, b, c, d = y[0:16], y[16:32], y[32:48], y[48:64]
    return jnp.concatenate([a * cr - b * sr, b * cr + a * sr, c * cc - d * sc, d * cc + c * sc], axis=0)


def _gqa_prep_kernel(x_ref, w_ref, qg_ref, kg_ref, tab_ref, qT_ref, k_ref, vT_ref):
    xb = x_ref[...].astype(BF16)
    accT = lax.dot_general(w_ref[...], xb, NT_DIMS, preferred_element_type=F32)
    tab = tab_ref[...]

    def normrope(blk, g):
        ms = jnp.mean(blk * blk, axis=0, keepdims=True)
        return _rope_t(blk * lax.rsqrt(ms + RMS_EPS) * g, tab)

    qg = qg_ref[...]
    kg = kg_ref[...]
    for h in range(N_HEADS):
        y = normrope(accT[HEAD_DIM * h:HEAD_DIM * (h + 1)], qg) * (HEAD_DIM ** -0.5 * LOG2E)
        qT_ref[HEAD_DIM * h:HEAD_DIM * (h + 1), :] = y.astype(BF16)
    k0 = N_HEADS * HEAD_DIM
    ks = [normrope(accT[k0 + HEAD_DIM * h:k0 + HEAD_DIM * (h + 1)], kg) for h in range(N_KV_HEADS)]
    k_ref[...] = jnp.concatenate(ks, axis=0).T.astype(BF16)
    v0 = k0 + N_KV_HEADS * HEAD_DIM
    vT_ref[...] = accT[v0:v0 + N_KV_HEADS * HEAD_DIM].astype(BF16)


def _gqa_prep(x, wT, qg, kg, tab):
    T = x.shape[0]
    t = TOK_TILE
    kvd = N_KV_HEADS * HEAD_DIM
    return pl.pallas_call(
        _gqa_prep_kernel,
        grid=(T // t,),
        in_specs=[
            pl.BlockSpec((t, D_MODEL), lambda i: (i, 0)),
            pl.BlockSpec((QKV_DIM, D_MODEL), lambda i: (0, 0)),
            pl.BlockSpec((HEAD_DIM, 1), lambda i: (0, 0)),
            pl.BlockSpec((HEAD_DIM, 1), lambda i: (0, 0)),
            pl.BlockSpec((HEAD_DIM, t), lambda i: (0, i)),
        ],
        out_specs=[
            pl.BlockSpec((D_MODEL, t), lambda i: (0, i)),
            pl.BlockSpec((t, kvd), lambda i: (i, 0)),
            pl.BlockSpec((kvd, t), lambda i: (0, i)),
        ],
        out_shape=[
            jax.ShapeDtypeStruct((D_MODEL, T), BF16),
            jax.ShapeDtypeStruct((T, kvd), BF16),
            jax.ShapeDtypeStruct((kvd, T), BF16),
        ],
        compiler_params=_cparams(("parallel",)),
        name="gqa_prep",
    )(x, wT, qg, kg, tab)


def _gqa_attn_kernel(kvs_ref, kvn_ref, q_ref, k_ref, vT_ref, o_ref):
    kh = pl.program_id(0)
    qi = pl.program_id(1)
    tq = q_ref.shape[1]
    q4 = q_ref[...]
    par = kh % 2
    zero = jnp.zeros((HEAD_DIM, tq), BF16)
    cols = []
    for g in range(GQA_GROUP):
        qg = q4[HEAD_DIM * g:HEAD_DIM * (g + 1)]
        cols.append(jnp.concatenate([jnp.where(par == 0, qg, zero), jnp.where(par == 1, qg, zero)], axis=0))
    qcat = jnp.concatenate(cols, axis=1)
    M = GQA_GROUP * tq
    start = kvs_ref[qi]
    nchunk = kvn_ref[qi]

    def body(j, carry):
        m, l, acc = carry
        off = pl.multiple_of(start + j * KV_CHUNK, KV_CHUNK)
        kc = k_ref[pl.ds(off, KV_CHUNK), :]
        sT = jnp.dot(kc, qcat, preferred_element_type=F32)
        m_new = jnp.maximum(m, jnp.max(sT, axis=0, keepdims=True))
        alpha = jnp.exp2(m - m_new)
        p = jnp.exp2(sT - m_new)
        l = alpha * l + jnp.sum(p, axis=0, keepdims=True)
        vc = vT_ref[:, pl.ds(off, KV_CHUNK)]
        acc = alpha * acc + jnp.dot(vc, p.astype(BF16), preferred_element_type=F32)
        return m_new, l, acc

    m0 = jnp.full((1, M), -jnp.inf, F32)
    l0 = jnp.zeros((1, M), F32)
    a0 = jnp.zeros((HEAD_DIM, M), F32)
    _, l, acc = lax.fori_loop(0, nchunk, body, (m0, l0, a0))
    oT = acc * (1.0 / l)
    o_ref[...] = jnp.concatenate([oT[:, g * tq:(g + 1) * tq] for g in range(GQA_GROUP)], axis=0).astype(BF16)


def _gqa_attn(qT, k, vT, kv_start, kv_chunks):
    T = qT.shape[1]
    tq = Q_TILE
    grid_spec = pltpu.PrefetchScalarGridSpec(
        num_scalar_prefetch=2,
        grid=(N_KV_HEADS, T // tq),
        in_specs=[
            pl.BlockSpec((GQA_GROUP * HEAD_DIM, tq), lambda kh, qi, a, b: (kh, qi)),
            pl.BlockSpec((T, 2 * HEAD_DIM), lambda kh, qi, a, b: (0, kh // 2)),
            pl.BlockSpec((HEAD_DIM, T), lambda kh, qi, a, b: (kh, 0)),
        ],
        out_specs=pl.BlockSpec((GQA_GROUP * HEAD_DIM, tq), lambda kh, qi, a, b: (kh, qi)),
    )
    return pl.pallas_call(
        _gqa_attn_kernel,
        grid_spec=grid_spec,
        out_shape=jax.ShapeDtypeStruct((D_MODEL, T), BF16),
        compiler_params=_cparams(("parallel", "parallel")),
        name="gqa_attn",
    )(kv_start, kv_chunks, qT, k, vT)


def _na_prep_kernel(x_ref, wqT_ref, wk_ref, wvT_ref, qT_ref, k_ref, vT_ref):
    xb = x_ref[...].astype(BF16)
    qT = lax.dot_general(wqT_ref[...], xb, NT_DIMS, preferred_element_type=F32)
    qT_ref[...] = (qT * (HEAD_DIM ** -0.5 * LOG2E)).astype(BF16)
    k_ref[...] = jnp.dot(xb, wk_ref[...], preferred_element_type=F32).astype(BF16)
    vT_ref[...] = lax.dot_general(wvT_ref[...], xb, NT_DIMS, preferred_element_type=F32).astype(BF16)


def _na_prep(x, wqT, wk, wvT):
    T = x.shape[0]
    t = TOK_TILE
    wspec = pl.BlockSpec((D_MODEL, D_MODEL), lambda i: (0, 0))
    return pl.pallas_call(
        _na_prep_kernel,
        grid=(T // t,),
        in_specs=[pl.BlockSpec((t, D_MODEL), lambda i: (i, 0)), wspec, wspec, wspec],
        out_specs=[
            pl.BlockSpec((D_MODEL, t), lambda i: (0, i)),
            pl.BlockSpec((t, D_MODEL), lambda i: (i, 0)),
            pl.BlockSpec((D_MODEL, t), lambda i: (0, i)),
        ],
        out_shape=[
            jax.ShapeDtypeStruct((D_MODEL, T), BF16),
            jax.ShapeDtypeStruct((T, D_MODEL), BF16),
            jax.ShapeDtypeStruct((D_MODEL, T), BF16),
        ],
        compiler_params=_cparams(("parallel",)),
        name="na_prep",
    )(x, wqT, wk, wvT)


def _na_attn_kernel(koff_ref, case_ref, q_ref, k_ref, vT_ref, bias_ref, o_ref):
    rb = pl.program_id(1)
    nq = q_ref.shape[1]
    nk = NA_KROWS * GRID_W
    off = pl.multiple_of(koff_ref[rb], 2 * HEAD_DIM)
    kc = k_ref[pl.ds(off, nk), :]
    q2 = q_ref[...]
    zero = jnp.zeros((HEAD_DIM, nq), BF16)
    for hh in range(2):
        qh = q2[HEAD_DIM * hh:HEAD_DIM * (hh + 1)]
        qpad = jnp.concatenate([qh, zero] if hh == 0 else [zero, qh], axis=0)
        sT = jnp.dot(kc, qpad, preferred_element_type=F32) + bias_ref[0, hh]
        m = jnp.max(sT, axis=0, keepdims=True)
        p = jnp.exp2(sT - m)
        l = jnp.sum(p, axis=0, keepdims=True)
        vc = vT_ref[HEAD_DIM * hh:HEAD_DIM * (hh + 1), pl.ds(off, nk)]
        oT = jnp.dot(vc, p.astype(BF16), preferred_element_type=F32) * (1.0 / l)
        o_ref[HEAD_DIM * hh:HEAD_DIM * (hh + 1), :] = oT.astype(BF16)


def _na_attn(qT, k, vT, bias, koff, case):
    T = qT.shape[1]
    nq = NA_QROWS * GRID_W
    nk = NA_KROWS * GRID_W
    grid_spec = pltpu.PrefetchScalarGridSpec(
        num_scalar_prefetch=2,
        grid=(N_HEADS // 2, T // nq),
        in_specs=[
            pl.BlockSpec((2 * HEAD_DIM, nq), lambda hp, rb, a, b: (hp, rb)),
            pl.BlockSpec((T, 2 * HEAD_DIM), lambda hp, rb, a, b: (0, hp)),
            pl.BlockSpec((2 * HEAD_DIM, T), lambda hp, rb, a, b: (hp, 0)),
            pl.BlockSpec((1, 2, nk, nq), lambda hp, rb, a, b: (b[rb], hp, 0, 0)),
        ],
        out_specs=pl.BlockSpec((2 * HEAD_DIM, nq), lambda hp, rb, a, b: (hp, rb)),
    )
    return pl.pallas_call(
        _na_attn_kernel,
        grid_spec=grid_spec,
        out_shape=jax.ShapeDtypeStruct((D_MODEL, T), BF16),
        compiler_params=_cparams(("parallel", "parallel")),
        name="na_attn",
    )(koff, case, qT, k, vT, bias)


def _na_bias_tables(rpb):
    R = 64
    a = np.arange(NA_QROWS)
    b = np.arange(NA_KROWS)
    row_oh = np.zeros((3, 2 * NA_ROWS - 1, NA_KROWS, NA_QROWS), np.float32)
    row_ok = np.zeros((3, NA_KROWS, NA_QROWS), bool)
    for c, r0 in enumerate((0, 24, R - NA_QROWS)):
        ws = min(max(r0 - NA_ROWS // 2, 0), R - NA_KROWS)
        r = r0 + a
        rs = np.clip(r - NA_ROWS // 2, 0, R - NA_ROWS)
        krow = ws + b
        rel = krow[:, None] - rs[None, :]
        ok = (rel >= 0) & (rel < NA_ROWS)
        dr = krow[:, None] - r[None, :] + (NA_ROWS - 1)
        row_ok[c] = ok
        for bi in range(NA_KROWS):
            for ai in range(NA_QROWS):
                if ok[bi, ai]:
                    row_oh[c, dr[bi, ai], bi, ai] = 1.0
    w = np.arange(GRID_W)
    cs = np.clip(w - NA_COLS // 2, 0, GRID_W - NA_COLS)
    kc = np.arange(GRID_W)
    relc = kc[:, None] - cs[None, :]
    col_ok = (relc >= 0) & (relc < NA_COLS)
    dc = kc[:, None] - w[None, :] + (NA_COLS - 1)
    col_oh = np.zeros((2 * NA_COLS - 1, GRID_W, GRID_W), np.float32)
    for ki in range(GRID_W):
        for wi in range(GRID_W):
            if col_ok[ki, wi]:
                col_oh[dc[ki, wi], ki, wi] = 1.0
    ok = row_ok[:, :, None, :, None] & col_ok[None, None, :, None, :]
    t = jnp.einsum("hrc,xrba,ckw->xhbkaw", rpb * LOG2E, jnp.asarray(row_oh), jnp.asarray(col_oh),
                   precision=lax.Precision.HIGHEST)
    t = jnp.where(jnp.asarray(ok)[:, None], t, NEG_BIG)
    return t.reshape(3, N_HEADS, NA_KROWS * GRID_W, NA_QROWS * GRID_W)


def _proj_ln_router_kernel(oT_ref, wo_ref, x_ref, g_ref, b_ref, wrT_ref, br_ref,
                           x1_ref, eid_ref, wts_ref, rank_ref, cnt_ref, carry_ref):
    i = pl.program_id(0)
    t = x_ref.shape[0]

    @pl.when(i == 0)
    def _():
        carry_ref[...] = jnp.zeros_like(carry_ref)

    h = lax.dot_general(oT_ref[...], wo_ref[...], TN_DIMS, preferred_element_type=F32)
    x1 = _layer_norm(DN_ALPHA * x_ref[...] + h, g_ref[...], b_ref[...])
    x1_ref[...] = x1

    xh = x1.astype(BF16)
    xl = (x1 - xh.astype(F32)).astype(BF16)
    wr = wrT_ref[...]
    wh = wr.astype(BF16)
    wl = (wr - wh.astype(F32)).astype(BF16)
    logits = (lax.dot_general(wh, xh, NT_DIMS, preferred_element_type=F32)
              + lax.dot_general(wh, xl, NT_DIMS, preferred_element_type=F32)
              + lax.dot_general(wl, xh, NT_DIMS, preferred_element_type=F32)) + br_ref[...]

    eidx = lax.broadcasted_iota(I32, (N_EXPERTS, t), 0).astype(F32)
    work = logits
    vals, ids, ohs = [], [], []
    for _ in range(TOP_K):
        mx = jnp.max(work, axis=0, keepdims=True)
        idx = jnp.min(jnp.where(work == mx, eidx, float(N_EXPERTS)), axis=0, keepdims=True)
        oh = eidx == idx
        vals.append(mx)
        ids.append(idx)
        ohs.append(oh)
        work = jnp.where(oh, -jnp.inf, work)
    es = [jnp.exp(v - vals[0]) for v in vals]
    inv = 1.0 / (es[0] + es[1] + es[2] + es[3])
    zrow = jnp.zeros((8 - TOP_K, t), F32)
    wts_ref[...] = jnp.concatenate([e * inv for e in es] + [zrow], axis=0)
    eid_ref[...] = jnp.concatenate(ids + [zrow], axis=0).astype(I32)

    oh_all = jnp.zeros((N_EXPERTS, t), F32)
    for oh in ohs:
        oh_all = oh_all + oh.astype(F32)
    tri = (lax.broadcasted_iota(I32, (t, t), 0) < lax.broadcasted_iota(I32, (t, t), 1)).astype(BF16)
    base = carry_ref[...] + jnp.dot(oh_all.astype(BF16), tri, preferred_element_type=F32)
    ranks = [jnp.sum(jnp.where(oh, base, 0.0), axis=0, keepdims=True) for oh in ohs]
    rank_ref[...] = jnp.concatenate(ranks + [zrow], axis=0).astype(I32)
    carry = carry_ref[...] + jnp.sum(oh_all, axis=1, keepdims=True)
    carry_ref[...] = carry
    cnt_ref[...] = jnp.broadcast_to(carry, cnt_ref.shape).astype(I32)


def _proj_ln_router(oT, wo, x, g, b, wrT, br):
    T = x.shape[0]
    t = TOK_TILE
    row = lambda i: (i, 0)
    col = lambda i: (0, i)
    fixed = lambda i: (0, 0)
    return pl.pallas_call(
        _proj_ln_router_kernel,
        grid=(T // t,),
        in_specs=[
            pl.BlockSpec((D_MODEL, t), col),
            pl.BlockSpec((D_MODEL, D_MODEL), fixed),
            pl.BlockSpec((t, D_MODEL), row),
            pl.BlockSpec((1, D_MODEL), fixed),
            pl.BlockSpec((1, D_MODEL), fixed),
            pl.BlockSpec((N_EXPERTS, D_MODEL), fixed),
            pl.BlockSpec((N_EXPERTS, 1), fixed),
        ],
        out_specs=[
            pl.BlockSpec((t, D_MODEL), row),
            pl.BlockSpec((8, t), col),
            pl.BlockSpec((8, t), col),
            pl.BlockSpec((8, t), col),
            pl.BlockSpec((N_EXPERTS, 128), fixed),
        ],
        out_shape=[
            jax.ShapeDtypeStruct((T, D_MODEL), F32),
            jax.ShapeDtypeStruct((8, T), I32),
            jax.ShapeDtypeStruct((8, T), F32),
            jax.ShapeDtypeStruct((8, T), I32),
            jax.ShapeDtypeStruct((N_EXPERTS, 128), I32),
        ],
        scratch_shapes=[pltpu.VMEM((N_EXPERTS, 1), F32)],
        compiler_params=_cparams(("arbitrary",)),
        name="proj_ln_router",
    )(oT, wo, x, g, b, wrT, br)


def _deinterleave_perm():
    p = np.zeros((256, 256), np.float32)
    k = np.arange(128)
    p[2 * k, k] = 1.0
    p[2 * k + 1, 128 + k] = 1.0
    return jnp.asarray(p, BF16)


def _wgu_prep_kernel(w_ref, perm_ref, wg_ref, wu_ref):
    w = w_ref[0].astype(BF16)
    perm = perm_ref[...]
    for c in range(2 * D_FF // 256):
        r = jnp.dot(w[:, 256 * c:256 * (c + 1)], perm, preferred_element_type=F32)
        wg_ref[0, :, 128 * c:128 * (c + 1)] = r[:, :128].astype(BF16)
        wu_ref[0, :, 128 * c:128 * (c + 1)] = r[:, 128:].astype(BF16)


def _wgu_prep(w_gu):
    n = w_gu.shape[0]
    rt = 512
    return pl.pallas_call(
        _wgu_prep_kernel,
        grid=(n, D_MODEL // rt),
        in_specs=[
            pl.BlockSpec((1, rt, 2 * D_FF), lambda e, i: (e, i, 0)),
            pl.BlockSpec((256, 256), lambda e, i: (0, 0)),
        ],
        out_specs=[
            pl.BlockSpec((1, rt, D_FF), lambda e, i: (e, i, 0)),
            pl.BlockSpec((1, rt, D_FF), lambda e, i: (e, i, 0)),
        ],
        out_shape=[
            jax.ShapeDtypeStruct((n, D_MODEL, D_FF), BF16),
            jax.ShapeDtypeStruct((n, D_MODEL, D_FF), BF16),
        ],
        compiler_params=_cparams(("parallel", "parallel")),
        name="wgu_prep",
    )(w_gu, _deinterleave_perm())


def _experts_kernel(te_ref, nused_ref, xs_ref, wg_ref, wu_ref, bg_ref, bu_ref, wd_ref, bd_ref, y_ref):
    i = pl.program_id(0)

    @pl.when(i < nused_ref[0])
    def _():
        x = xs_ref[...]
        g = jnp.dot(x, wg_ref[0], preferred_element_type=F32) + bg_ref[0]
        u = jnp.dot(x, wu_ref[0], preferred_element_type=F32) + bu_ref[0]
        g = jnp.minimum(g, SWIGLU_LIMIT)
        u = jnp.clip(u, -SWIGLU_LIMIT, SWIGLU_LIMIT)
        a = (u + 1.0) * (g * jax.nn.sigmoid(SWIGLU_ALPHA * g))
        y = jnp.dot(a.astype(BF16), wd_ref[0], preferred_element_type=F32) + bd_ref[0]
        y_ref[...] = y.astype(BF16)

    @pl.when(i >= nused_ref[0])
    def _():
        y_ref[...] = jnp.zeros_like(y_ref)


def _experts(tile_expert, n_used, xs, wg, wu, bg, bu, wd, bd):
    P = xs.shape[0]
    r = EXP_TILE
    wmap = lambda i, te, nu: (te[i], 0, 0)
    grid_spec = pltpu.PrefetchScalarGridSpec(
        num_scalar_prefetch=2,
        grid=(P // r,),
        in_specs=[
            pl.BlockSpec((r, D_MODEL), lambda i, te, nu: (i, 0)),
            pl.BlockSpec((1, D_MODEL, D_FF), wmap),
            pl.BlockSpec((1, D_MODEL, D_FF), wmap),
            pl.BlockSpec((1, 1, D_FF), wmap),
            pl.BlockSpec((1, 1, D_FF), wmap),
            pl.BlockSpec((1, D_FF, D_MODEL), wmap),
            pl.BlockSpec((1, 1, D_MODEL), wmap),
        ],
        out_specs=pl.BlockSpec((r, D_MODEL), lambda i, te, nu: (i, 0)),
    )
    return pl.pallas_call(
        _experts_kernel,
        grid_spec=grid_spec,
        out_shape=jax.ShapeDtypeStruct((P, D_MODEL), BF16),
        compiler_params=_cparams(("parallel",)),
        name="experts",
    )(tile_expert, n_used, xs, wg, wu, bg, bu, wd, bd)


def _combine_ln_kernel(yg_ref, wt_ref, x_ref, g_ref, b_ref, o_ref):
    wt = wt_ref[...]
    f = yg_ref[0].astype(F32) * wt[:, 0:1]
    for r in range(1, TOP_K):
        f = f + yg_ref[r].astype(F32) * wt[:, r:r + 1]
    o_ref[...] = _layer_norm(DN_ALPHA * x_ref[...] + f, g_ref[...], b_ref[...])


def _combine_ln(yg, wt, x, g, b):
    T = x.shape[0]
    t = TOK_TILE
    return pl.pallas_call(
        _combine_ln_kernel,
        grid=(T // t,),
        in_specs=[
            pl.BlockSpec((TOP_K, t, D_MODEL), lambda i: (0, i, 0)),
            pl.BlockSpec((t, TOP_K), lambda i: (i, 0)),
            pl.BlockSpec((t, D_MODEL), lambda i: (i, 0)),
            pl.BlockSpec((1, D_MODEL), lambda i: (0, 0)),
            pl.BlockSpec((1, D_MODEL), lambda i: (0, 0)),
        ],
        out_specs=pl.BlockSpec((t, D_MODEL), lambda i: (i, 0)),
        out_shape=jax.ShapeDtypeStruct((T, D_MODEL), F32),
        compiler_params=_cparams(("parallel",)),
        name="combine_ln",
    )(yg, wt, x, g, b)


def _moe(x1, eid, rank, counts, layer, wg, wu, bg, bu, wd, bd):
    T = x1.shape[0]
    r = EXP_TILE
    P = T * TOP_K + N_EXPERTS * r
    n_tiles = P // r
    eid = eid[:TOP_K]
    padded = ((counts + (r - 1)) // r) * r
    ends = jnp.cumsum(padded)
    starts = ends - padded
    pos = starts[eid] + rank[:TOP_K]
    tile_expert = jnp.minimum(
        jnp.searchsorted(ends, jnp.arange(n_tiles, dtype=I32) * r, side="right"), N_EXPERTS - 1).astype(I32)
    n_used = (ends[-1] // r).astype(I32).reshape(1)
    tok = jnp.tile(jnp.arange(T, dtype=I32), TOP_K)
    tok_of_row = jnp.zeros((P,), I32).at[pos.reshape(-1)].set(tok)
    xs = jnp.take(x1.astype(BF16), tok_of_row, axis=0)
    y = _experts(tile_expert + layer * N_EXPERTS, n_used, xs, wg, wu, bg, bu, wd, bd)
    return jnp.take(y, pos, axis=0)


def _rope_table(seqs, T):
    pos = np.zeros((T,), np.int32)
    for s0, n in seqs:
        pos[s0:s0 + n] = np.arange(n)
    pos = jnp.asarray(pos)
    row = (pos // GRID_W).astype(F32)
    col = (pos % GRID_W).astype(F32)
    freqs = ROPE_THETA ** (-jnp.arange(0, AXIS_DIM, 2, dtype=F32) / AXIS_DIM)
    ar = freqs[:, None] * row[None, :]
    ac = freqs[:, None] * col[None, :]
    return jnp.concatenate([jnp.cos(ar), jnp.sin(ar), jnp.cos(ac), jnp.sin(ac)], axis=0)


def _trunk(x, seqs, gqa_w_qkv, gqa_q_norm, gqa_k_norm, gqa_w_o, na_w_qkv, na_rpb, na_w_o,
           ln1_g, ln1_b, ln2_g, ln2_b, router_w, router_b, exp_w_gu, exp_b_gu, exp_w_down, exp_b_down):
    T = x.shape[0]
    tab = _rope_table(seqs, T)
    kv_start = np.zeros((T // Q_TILE,), np.int32)
    kv_chunks = np.zeros((T // Q_TILE,), np.int32)
    nq = NA_QROWS * GRID_W
    na_koff = np.zeros((T // nq,), np.int32)
    na_case = np.ones((T // nq,), np.int32)
    for s0, n in seqs:
        kv_start[s0 // Q_TILE:(s0 + n) // Q_TILE] = s0
        kv_chunks[s0 // Q_TILE:(s0 + n) // Q_TILE] = n // KV_CHUNK
        rows = n // GRID_W
        for r0 in range(0, rows, NA_QROWS):
            ws = min(max(r0 - NA_ROWS // 2, 0), rows - NA_KROWS)
            rb = (s0 + r0 * GRID_W) // nq
            na_koff[rb] = s0 + ws * GRID_W
            na_case[rb] = 0 if r0 == 0 else (2 if r0 == rows - NA_QROWS else 1)
    kv_start, kv_chunks = jnp.asarray(kv_start), jnp.asarray(kv_chunks)
    na_koff, na_case = jnp.asarray(na_koff), jnp.asarray(na_case)

    ne = DEPTH * N_EXPERTS
    wg_all, wu_all = _wgu_prep(exp_w_gu.reshape(ne, D_MODEL, 2 * D_FF))
    bgu = exp_b_gu.reshape(ne, 1, 2 * D_FF)
    bg_all, bu_all = bgu[..., 0::2], bgu[..., 1::2]
    wd_all = exp_w_down.reshape(ne, D_FF, D_MODEL).astype(BF16)
    bd_all = exp_b_down.reshape(ne, 1, D_MODEL)

    for i in range(DEPTH):
        j = i // 2
        if i % 2 == 0:
            wT = gqa_w_qkv[j].T.astype(BF16)
            qT, k, vT = _gqa_prep(x, wT, gqa_q_norm[j][:, None], gqa_k_norm[j][:, None], tab)
            oT = _gqa_attn(qT, k, vT, kv_start, kv_chunks)
            wo = gqa_w_o[j].astype(BF16)
        else:
            w = na_w_qkv[j]
            qT, k, vT = _na_prep(x, w[:, :D_MODEL].T.astype(BF16), w[:, D_MODEL:2 * D_MODEL].astype(BF16),
                                 w[:, 2 * D_MODEL:].T.astype(BF16))
            oT = _na_attn(qT, k, vT, _na_bias_tables(na_rpb[j]), na_koff, na_case)
            wo = na_w_o[j].astype(BF16)
        x1, eid, wts, rank, cnt = _proj_ln_router(oT, wo, x, ln1_g[i][None], ln1_b[i][None],
                                                  router_w[i].T, router_b[i][:, None])
        yg = _moe(x1, eid, rank, cnt[:, 0], i, wg_all, wu_all, bg_all, bu_all, wd_all, bd_all)
        x = _combine_ln(yg, wts[:TOP_K].T, x1, ln2_g[i][None], ln2_b[i][None])
    return x


def kernel(x_prompt, x_sample, gqa_w_qkv, gqa_q_norm, gqa_k_norm, gqa_w_o, na_w_qkv, na_rpb, na_w_o, ln1_g, ln1_b, ln2_g, ln2_b, router_w, router_b, exp_w_gu, exp_b_gu, exp_w_down, exp_b_down):
    bp, sp, d = x_prompt.shape
    bs, ss, _ = x_sample.shape
    x = jnp.concatenate([x_prompt.reshape(bp * sp, d), x_sample.reshape(bs * ss, d)], axis=0)
    seqs = tuple((b * sp, sp) for b in range(bp)) + tuple((bp * sp + b * ss, ss) for b in range(bs))
    y = _trunk(x, seqs, gqa_w_qkv, gqa_q_norm, gqa_k_norm, gqa_w_o, na_w_qkv, na_rpb, na_w_o,
               ln1_g, ln1_b, ln2_g, ln2_b, router_w, router_b, exp_w_gu, exp_b_gu, exp_w_down, exp_b_down)
    return (y[:bp * sp].reshape(bp, sp, d), y[bp * sp:].reshape(bs, ss, d))
```

```python
import functools
import math

import numpy as np
import jax
import jax.numpy as jnp
from jax import lax
from jax.experimental import pallas as pl
from jax.experimental.pallas import tpu as pltpu

F32 = jnp.float32
BF16 = jnp.bfloat16
I32 = jnp.int32

D_MODEL = 1024
DEPTH = 4
GRID_W = 64
HEAD_DIM = 64
N_HEADS = 16
N_KV_HEADS = 4
GQA_GROUP = 4
QKV_DIM = 1536
ROPE_THETA = 10000.0
AXIS_DIM = 32
NA_ROWS = 8
NA_COLS = 16
N_EXPERTS = 32
TOP_K = 4
D_FF = 1024
SWIGLU_LIMIT = 7.0
SWIGLU_ALPHA = 1.702
DN_ALPHA = (2 * DEPTH) ** 0.25
LN_EPS = 1e-5
RMS_EPS = 1e-6
LOG2E = 1.4426950408889634
NEG_BIG = -1e30

VMEM_LIMIT = 56 * 1024 * 1024
TOK_TILE = 512
Q_TILE = 256
KV_CHUNK = 256
NA_QROWS = 8
NA_KROWS = 16
EXP_TILE = 512

NT_DIMS = (((1,), (1,)), ((), ()))
TN_DIMS = (((0,), (0,)), ((), ()))


def _cparams(sem):
    return pltpu.CompilerParams(dimension_semantics=sem, vmem_limit_bytes=VMEM_LIMIT)


def _layer_norm(y, g, b):
    mu = jnp.mean(y, axis=-1, keepdims=True)
    yc = y - mu
    var = jnp.mean(yc * yc, axis=-1, keepdims=True)
    return yc * lax.rsqrt(var + LN_EPS) * g + b


def _rope_t(y, tab):
    cr, sr, cc, sc = tab[0:16], tab[16:32], tab[32:48], tab[48:64]
    a, b, c, d = y[0:16], y[16:32], y[32:48], y[48:64]
    return jnp.concatenate([a * cr - b * sr, b * cr + a * sr, c * cc - d * sc, d * cc + c * sc], axis=0)


def _gqa_prep_kernel(x_ref, w_ref, qg_ref, kg_ref, tab_ref, qT_ref, k_ref, vT_ref):
    xb = x_ref[...].astype(BF16)
    accT = lax.dot_general(w_ref[...], xb, NT_DIMS, preferred_element_type=F32)
    tab = tab_ref[...]

    def normrope(blk, g):
        ms = jnp.mean(blk * blk, axis=0, keepdims=True)
        return _rope_t(blk * lax.rsqrt(ms + RMS_EPS) * g, tab)

    qg = qg_ref[...]
    kg = kg_ref[...]
    for h in range(N_HEADS):
        y = normrope(accT[HEAD_DIM * h:HEAD_DIM * (h + 1)], qg) * (HEAD_DIM ** -0.5 * LOG2E)
        qT_ref[HEAD_DIM * h:HEAD_DIM * (h + 1), :] = y.astype(BF16)
    k0 = N_HEADS * HEAD_DIM
    ks = [normrope(accT[k0 + HEAD_DIM * h:k0 + HEAD_DIM * (h + 1)], kg) for h in range(N_KV_HEADS)]
    k_ref[...] = jnp.concatenate(ks, axis=0).T.astype(BF16)
    v0 = k0 + N_KV_HEADS * HEAD_DIM
    vT_ref[...] = accT[v0:v0 + N_KV_HEADS * HEAD_DIM].astype(BF16)


def _gqa_prep(x, wT, qg, kg, tab):
    T = x.shape[0]
    t = TOK_TILE
    kvd = N_KV_HEADS * HEAD_DIM
    return pl.pallas_call(
        _gqa_prep_kernel,
        grid=(T // t,),
        in_specs=[
            pl.BlockSpec((t, D_MODEL), lambda i: (i, 0)),
            pl.BlockSpec((QKV_DIM, D_MODEL), lambda i: (0, 0)),
            pl.BlockSpec((HEAD_DIM, 1), lambda i: (0, 0)),
            pl.BlockSpec((HEAD_DIM, 1), lambda i: (0, 0)),
            pl.BlockSpec((HEAD_DIM, t), lambda i: (0, i)),
        ],
        out_specs=[
            pl.BlockSpec((D_MODEL, t), lambda i: (0, i)),
            pl.BlockSpec((t, kvd), lambda i: (i, 0)),
            pl.BlockSpec((kvd, t), lambda i: (0, i)),
        ],
        out_shape=[
            jax.ShapeDtypeStruct((D_MODEL, T), BF16),
            jax.ShapeDtypeStruct((T, kvd), BF16),
            jax.ShapeDtypeStruct((kvd, T), BF16),
        ],
        compiler_params=_cparams(("parallel",)),
        name="gqa_prep",
    )(x, wT, qg, kg, tab)


def _gqa_attn_kernel(kvs_ref, kvn_ref, q_ref, k_ref, vT_ref, o_ref, s_ref):
    kh = pl.program_id(0)
    qi = pl.program_id(1)
    tq = q_ref.shape[1]
    q4 = q_ref[...]
    par = kh % 2
    zero = jnp.zeros((HEAD_DIM, tq), BF16)
    cols = []
    for g in range(GQA_GROUP):
        qg = q4[HEAD_DIM * g:HEAD_DIM * (g + 1)]
        cols.append(jnp.concatenate([jnp.where(par == 0, qg, zero), jnp.where(par == 1, qg, zero)], axis=0))
    qcat = jnp.concatenate(cols, axis=1)
    M = GQA_GROUP * tq
    start = kvs_ref[qi]
    nchunk = kvn_ref[qi]

    def scores(slot, c):
        off = pl.multiple_of(start + c * KV_CHUNK, KV_CHUNK)
        kc = k_ref[pl.ds(off, KV_CHUNK), :]
        s_ref[slot] = jnp.dot(kc, qcat, preferred_element_type=F32)

    def softmax_pv(slot, c, m, l, acc):
        sT = s_ref[slot]
        m_new = jnp.maximum(m, jnp.max(sT, axis=0, keepdims=True))
        alpha = jnp.exp2(m - m_new)
        p = jnp.exp2(sT - m_new)
        l = alpha * l + jnp.sum(p, axis=0, keepdims=True)
        off = pl.multiple_of(start + c * KV_CHUNK, KV_CHUNK)
        vc = vT_ref[:, pl.ds(off, KV_CHUNK)]
        acc = alpha * acc + jnp.dot(vc, p.astype(BF16), preferred_element_type=F32)
        return m_new, l, acc

    scores(0, 0)

    def body(j, carry):
        m, l, acc = carry
        c0 = 2 * j
        scores(1, c0 + 1)
        m, l, acc = softmax_pv(0, c0, m, l, acc)
        scores(0, jnp.minimum(c0 + 2, nchunk - 1))
        return softmax_pv(1, c0 + 1, m, l, acc)

    m0 = jnp.full((1, M), -jnp.inf, F32)
    l0 = jnp.zeros((1, M), F32)
    a0 = jnp.zeros((HEAD_DIM, M), F32)
    _, l, acc = lax.fori_loop(0, nchunk // 2, body, (m0, l0, a0))
    oT = acc * (1.0 / l)
    o_ref[...] = jnp.concatenate([oT[:, g * tq:(g + 1) * tq] for g in range(GQA_GROUP)], axis=0).astype(BF16)


def _gqa_attn(qT, k, vT, kv_start, kv_chunks):
    T = qT.shape[1]
    tq = Q_TILE
    grid_spec = pltpu.PrefetchScalarGridSpec(
        num_scalar_prefetch=2,
        grid=(N_KV_HEADS, T // tq),
        in_specs=[
            pl.BlockSpec((GQA_GROUP * HEAD_DIM, tq), lambda kh, qi, a, b: (kh, qi)),
            pl.BlockSpec((T, 2 * HEAD_DIM), lambda kh, qi, a, b: (0, kh // 2)),
            pl.BlockSpec((HEAD_DIM, T), lambda kh, qi, a, b: (kh, 0)),
        ],
        out_specs=pl.BlockSpec((GQA_GROUP * HEAD_DIM, tq), lambda kh, qi, a, b: (kh, qi)),
        scratch_shapes=[pltpu.VMEM((2, KV_CHUNK, GQA_GROUP * tq), F32)],
    )
    return pl.pallas_call(
        _gqa_attn_kernel,
        grid_spec=grid_spec,
        out_shape=jax.ShapeDtypeStruct((D_MODEL, T), BF16),
        compiler_params=_cparams(("parallel", "parallel")),
        name="gqa_attn",
    )(kv_start, kv_chunks, qT, k, vT)


def _na_prep_kernel(x_ref, wqT_ref, wk_ref, wvT_ref, qT_ref, k_ref, vT_ref):
    xb = x_ref[...].astype(BF16)
    qT = lax.dot_general(wqT_ref[...], xb, NT_DIMS, preferred_element_type=F32)
    qT_ref[...] = (qT * (HEAD_DIM ** -0.5 * LOG2E)).astype(BF16)
    k_ref[...] = jnp.dot(xb, wk_ref[...], preferred_element_type=F32).astype(BF16)
    vT_ref[...] = lax.dot_general(wvT_ref[...], xb, NT_DIMS, preferred_element_type=F32).astype(BF16)


def _na_prep(x, wqT, wk, wvT):
    T = x.shape[0]
    t = TOK_TILE
    wspec = pl.BlockSpec((D_MODEL, D_MODEL), lambda i: (0, 0))
    return pl.pallas_call(
        _na_prep_kernel,
        grid=(T // t,),
        in_specs=[pl.BlockSpec((t, D_MODEL), lambda i: (i, 0)), wspec, wspec, wspec],
        out_specs=[
            pl.BlockSpec((D_MODEL, t), lambda i: (0, i)),
            pl.BlockSpec((t, D_MODEL), lambda i: (i, 0)),
            pl.BlockSpec((D_MODEL, t), lambda i: (0, i)),
        ],
        out_shape=[
            jax.ShapeDtypeStruct((D_MODEL, T), BF16),
            jax.ShapeDtypeStruct((T, D_MODEL), BF16),
            jax.ShapeDtypeStruct((D_MODEL, T), BF16),
        ],
        compiler_params=_cparams(("parallel",)),
        name="na_prep",
    )(x, wqT, wk, wvT)


def _na_attn_kernel(koff_ref, case_ref, q_ref, k_ref, vT_ref, bias_ref, o_ref):
    rb = pl.program_id(1)
    nq = q_ref.shape[1]
    nk = NA_KROWS * GRID_W
    off = pl.multiple_of(koff_ref[rb], 2 * HEAD_DIM)
    kc = k_ref[pl.ds(off, nk), :]
    q2 = q_ref[...]
    zero = jnp.zeros((HEAD_DIM, nq), BF16)
    for hh in range(2):
        qh = q2[HEAD_DIM * hh:HEAD_DIM * (hh + 1)]
        qpad = jnp.concatenate([qh, zero] if hh == 0 else [zero, qh], axis=0)
        sT = jnp.dot(kc, qpad, preferred_element_type=F32) + bias_ref[0, hh]
        m = jnp.max(sT, axis=0, keepdims=True)
        p = jnp.exp2(sT - m)
        l = jnp.sum(p, axis=0, keepdims=True)
        vc = vT_ref[HEAD_DIM * hh:HEAD_DIM * (hh + 1), pl.ds(off, nk)]
        oT = jnp.dot(vc, p.astype(BF16), preferred_element_type=F32) * (1.0 / l)
        o_ref[HEAD_DIM * hh:HEAD_DIM * (hh + 1), :] = oT.astype(BF16)


def _na_attn(qT, k, vT, bias, koff, case):
    T = qT.shape[1]
    nq = NA_QROWS * GRID_W
    nk = NA_KROWS * GRID_W
    grid_spec = pltpu.PrefetchScalarGridSpec(
        num_scalar_prefetch=2,
        grid=(N_HEADS // 2, T // nq),
        in_specs=[
            pl.BlockSpec((2 * HEAD_DIM, nq), lambda hp, rb, a, b: (hp, rb)),
            pl.BlockSpec((T, 2 * HEAD_DIM), lambda hp, rb, a, b: (0, hp)),
            pl.BlockSpec((2 * HEAD_DIM, T), lambda hp, rb, a, b: (hp, 0)),
            pl.BlockSpec((1, 2, nk, nq), lambda hp, rb, a, b: (b[rb], hp, 0, 0)),
        ],
        out_specs=pl.BlockSpec((2 * HEAD_DIM, nq), lambda hp, rb, a, b: (hp, rb)),
    )
    return pl.pallas_call(
        _na_attn_kernel,
        grid_spec=grid_spec,
        out_shape=jax.ShapeDtypeStruct((D_MODEL, T), BF16),
        compiler_params=_cparams(("parallel", "parallel")),
        name="na_attn",
    )(koff, case, qT, k, vT, bias)


def _na_bias_tables(rpb):
    R = 64
    a = np.arange(NA_QROWS)
    b = np.arange(NA_KROWS)
    row_oh = np.zeros((3, 2 * NA_ROWS - 1, NA_KROWS, NA_QROWS), np.float32)
    row_ok = np.zeros((3, NA_KROWS, NA_QROWS), bool)
    for c, r0 in enumerate((0, 24, R - NA_QROWS)):
        ws = min(max(r0 - NA_ROWS // 2, 0), R - NA_KROWS)
        r = r0 + a
        rs = np.clip(r - NA_ROWS // 2, 0, R - NA_ROWS)
        krow = ws + b
        rel = krow[:, None] - rs[None, :]
        ok = (rel >= 0) & (rel < NA_ROWS)
        dr = krow[:, None] - r[None, :] + (NA_ROWS - 1)
        row_ok[c] = ok
        for bi in range(NA_KROWS):
            for ai in range(NA_QROWS):
                if ok[bi, ai]:
                    row_oh[c, dr[bi, ai], bi, ai] = 1.0
    w = np.arange(GRID_W)
    cs = np.clip(w - NA_COLS // 2, 0, GRID_W - NA_COLS)
    kc = np.arange(GRID_W)
    relc = kc[:, None] - cs[None, :]
    col_ok = (relc >= 0) & (relc < NA_COLS)
    dc = kc[:, None] - w[None, :] + (NA_COLS - 1)
    col_oh = np.zeros((2 * NA_COLS - 1, GRID_W, GRID_W), np.float32)
    for ki in range(GRID_W):
        for wi in range(GRID_W):
            if col_ok[ki, wi]:
                col_oh[dc[ki, wi], ki, wi] = 1.0
    ok = row_ok[:, :, None, :, None] & col_ok[None, None, :, None, :]
    t = jnp.einsum("hrc,xrba,ckw->xhbkaw", rpb * LOG2E, jnp.asarray(row_oh), jnp.asarray(col_oh),
                   precision=lax.Precision.HIGHEST)
    t = jnp.where(jnp.asarray(ok)[:, None], t, NEG_BIG)
    return t.reshape(3, N_HEADS, NA_KROWS * GRID_W, NA_QROWS * GRID_W)


def _proj_ln_router_kernel(oT_ref, wo_ref, x_ref, g_ref, b_ref, wrT_ref, br_ref,
                           x1_ref, eid_ref, wts_ref, rank_ref, cnt_ref, carry_ref):
    i = pl.program_id(0)
    t = x_ref.shape[0]

    @pl.when(i == 0)
    def _():
        carry_ref[...] = jnp.zeros_like(carry_ref)

    h = lax.dot_general(oT_ref[...], wo_ref[...], TN_DIMS, preferred_element_type=F32)
    x1 = _layer_norm(DN_ALPHA * x_ref[...] + h, g_ref[...], b_ref[...])
    x1_ref[...] = x1

    xh = x1.astype(BF16)
    xl = (x1 - xh.astype(F32)).astype(BF16)
    wr = wrT_ref[...]
    wh = wr.astype(BF16)
    wl = (wr - wh.astype(F32)).astype(BF16)
    logits = (lax.dot_general(wh, xh, NT_DIMS, preferred_element_type=F32)
              + lax.dot_general(wh, xl, NT_DIMS, preferred_element_type=F32)
              + lax.dot_general(wl, xh, NT_DIMS, preferred_element_type=F32)) + br_ref[...]

    eidx = lax.broadcasted_iota(I32, (N_EXPERTS, t), 0).astype(F32)
    work = logits
    vals, ids, ohs = [], [], []
    for _ in range(TOP_K):
        mx = jnp.max(work, axis=0, keepdims=True)
        idx = jnp.min(jnp.where(work == mx, eidx, float(N_EXPERTS)), axis=0, keepdims=True)
        oh = eidx == idx
        vals.append(mx)
        ids.append(idx)
        ohs.append(oh)
        work = jnp.where(oh, -jnp.inf, work)
    es = [jnp.exp(v - vals[0]) for v in vals]
    inv = 1.0 / (es[0] + es[1] + es[2] + es[3])
    zrow = jnp.zeros((8 - TOP_K, t), F32)
    wts_ref[...] = jnp.concatenate([e * inv for e in es] + [zrow], axis=0)
    eid_ref[...] = jnp.concatenate(ids + [zrow], axis=0).astype(I32)

    oh_all = jnp.zeros((N_EXPERTS, t), F32)
    for oh in ohs:
        oh_all = oh_all + oh.astype(F32)
    tri = (lax.broadcasted_iota(I32, (t, t), 0) < lax.broadcasted_iota(I32, (t, t), 1)).astype(BF16)
    base = carry_ref[...] + jnp.dot(oh_all.astype(BF16), tri, preferred_element_type=F32)
    ranks = [jnp.sum(jnp.where(oh, base, 0.0), axis=0, keepdims=True) for oh in ohs]
    rank_ref[...] = jnp.concatenate(ranks + [zrow], axis=0).astype(I32)
    carry = carry_ref[...] + jnp.sum(oh_all, axis=1, keepdims=True)
    carry_ref[...] = carry
    cnt_ref[...] = jnp.broadcast_to(carry, cnt_ref.shape).astype(I32)


def _proj_ln_router(oT, wo, x, g, b, wrT, br):
    T = x.shape[0]
    t = TOK_TILE
    row = lambda i: (i, 0)
    col = lambda i: (0, i)
    fixed = lambda i: (0, 0)
    return pl.pallas_call(
        _proj_ln_router_kernel,
        grid=(T // t,),
        in_specs=[
            pl.BlockSpec((D_MODEL, t), col),
            pl.BlockSpec((D_MODEL, D_MODEL), fixed),
            pl.BlockSpec((t, D_MODEL), row),
            pl.BlockSpec((1, D_MODEL), fixed),
            pl.BlockSpec((1, D_MODEL), fixed),
            pl.BlockSpec((N_EXPERTS, D_MODEL), fixed),
            pl.BlockSpec((N_EXPERTS, 1), fixed),
        ],
        out_specs=[
            pl.BlockSpec((t, D_MODEL), row),
            pl.BlockSpec((8, t), col),
            pl.BlockSpec((8, t), col),
            pl.BlockSpec((8, t), col),
            pl.BlockSpec((N_EXPERTS, 128), fixed),
        ],
        out_shape=[
            jax.ShapeDtypeStruct((T, D_MODEL), F32),
            jax.ShapeDtypeStruct((8, T), I32),
            jax.ShapeDtypeStruct((8, T), F32),
            jax.ShapeDtypeStruct((8, T), I32),
            jax.ShapeDtypeStruct((N_EXPERTS, 128), I32),
        ],
        scratch_shapes=[pltpu.VMEM((N_EXPERTS, 1), F32)],
        compiler_params=_cparams(("arbitrary",)),
        name="proj_ln_router",
    )(oT, wo, x, g, b, wrT, br)


def _deinterleave_perm():
    p = np.zeros((256, 256), np.float32)
    k = np.arange(128)
    p[2 * k, k] = 1.0
    p[2 * k + 1, 128 + k] = 1.0
    return jnp.asarray(p, BF16)


def _wgu_prep_kernel(w_ref, perm_ref, wg_ref, wu_ref):
    w = w_ref[0].astype(BF16)
    perm = perm_ref[...]
    for c in range(2 * D_FF // 256):
        r = jnp.dot(w[:, 256 * c:256 * (c + 1)], perm, preferred_element_type=F32)
        wg_ref[0, :, 128 * c:128 * (c + 1)] = r[:, :128].astype(BF16)
        wu_ref[0, :, 128 * c:128 * (c + 1)] = r[:, 128:].astype(BF16)


def _wgu_prep(w_gu):
    n = w_gu.shape[0]
    rt = 512
    return pl.pallas_call(
        _wgu_prep_kernel,
        grid=(n, D_MODEL // rt),
        in_specs=[
            pl.BlockSpec((1, rt, 2 * D_FF), lambda e, i: (e, i, 0)),
            pl.BlockSpec((256, 256), lambda e, i: (0, 0)),
        ],
        out_specs=[
            pl.BlockSpec((1, rt, D_FF), lambda e, i: (e, i, 0)),
            pl.BlockSpec((1, rt, D_FF), lambda e, i: (e, i, 0)),
        ],
        out_shape=[
            jax.ShapeDtypeStruct((n, D_MODEL, D_FF), BF16),
            jax.ShapeDtypeStruct((n, D_MODEL, D_FF), BF16),
        ],
        compiler_params=_cparams(("parallel", "parallel")),
        name="wgu_prep",
    )(w_gu, _deinterleave_perm())


def _experts_kernel(te_ref, nused_ref, xs_ref, wg_ref, wu_ref, bg_ref, bu_ref, wd_ref, bd_ref, y_ref):
    i = pl.program_id(0)

    @pl.when(i < nused_ref[0])
    def _():
        x = xs_ref[...]
        g = jnp.dot(x, wg_ref[0], preferred_element_type=F32) + bg_ref[0]
        u = jnp.dot(x, wu_ref[0], preferred_element_type=F32) + bu_ref[0]
        g = jnp.minimum(g, SWIGLU_LIMIT)
        u = jnp.clip(u, -SWIGLU_LIMIT, SWIGLU_LIMIT)
        a = (u + 1.0) * (g * jax.nn.sigmoid(SWIGLU_ALPHA * g))
        y = jnp.dot(a.astype(BF16), wd_ref[0], preferred_element_type=F32) + bd_ref[0]
        y_ref[...] = y.astype(BF16)

    @pl.when(i >= nused_ref[0])
    def _():
        y_ref[...] = jnp.zeros_like(y_ref)


def _experts(tile_expert, n_used, xs, wg, wu, bg, bu, wd, bd):
    P = xs.shape[0]
    r = EXP_TILE
    wmap = lambda i, te, nu: (te[i], 0, 0)
    grid_spec = pltpu.PrefetchScalarGridSpec(
        num_scalar_prefetch=2,
        grid=(P // r,),
        in_specs=[
            pl.BlockSpec((r, D_MODEL), lambda i, te, nu: (i, 0)),
            pl.BlockSpec((1, D_MODEL, D_FF), wmap),
            pl.BlockSpec((1, D_MODEL, D_FF), wmap),
            pl.BlockSpec((1, 1, D_FF), wmap),
            pl.BlockSpec((1, 1, D_FF), wmap),
            pl.BlockSpec((1, D_FF, D_MODEL), wmap),
            pl.BlockSpec((1, 1, D_MODEL), wmap),
        ],
        out_specs=pl.BlockSpec((r, D_MODEL), lambda i, te, nu: (i, 0)),
    )
    return pl.pallas_call(
        _experts_kernel,
        grid_spec=grid_spec,
        out_shape=jax.ShapeDtypeStruct((P, D_MODEL), BF16),
        compiler_params=_cparams(("parallel",)),
        name="experts",
    )(tile_expert, n_used, xs, wg, wu, bg, bu, wd, bd)


def _combine_ln_kernel(yg_ref, wt_ref, x_ref, g_ref, b_ref, o_ref):
    wt = wt_ref[...]
    f = yg_ref[0].astype(F32) * wt[:, 0:1]
    for r in range(1, TOP_K):
        f = f + yg_ref[r].astype(F32) * wt[:, r:r + 1]
    o_ref[...] = _layer_norm(DN_ALPHA * x_ref[...] + f, g_ref[...], b_ref[...])


def _combine_ln(yg, wt, x, g, b):
    T = x.shape[0]
    t = TOK_TILE
    return pl.pallas_call(
        _combine_ln_kernel,
        grid=(T // t,),
        in_specs=[
            pl.BlockSpec((TOP_K, t, D_MODEL), lambda i: (0, i, 0)),
            pl.BlockSpec((t, TOP_K), lambda i: (i, 0)),
            pl.BlockSpec((t, D_MODEL), lambda i: (i, 0)),
            pl.BlockSpec((1, D_MODEL), lambda i: (0, 0)),
            pl.BlockSpec((1, D_MODEL), lambda i: (0, 0)),
        ],
        out_specs=pl.BlockSpec((t, D_MODEL), lambda i: (i, 0)),
        out_shape=jax.ShapeDtypeStruct((T, D_MODEL), F32),
        compiler_params=_cparams(("parallel",)),
        name="combine_ln",
    )(yg, wt, x, g, b)


def _moe(x1, eid, rank, counts, layer, wg, wu, bg, bu, wd, bd):
    T = x1.shape[0]
    r = EXP_TILE
    P = T * TOP_K + N_EXPERTS * r
    n_tiles = P // r
    eid = eid[:TOP_K]
    padded = ((counts + (r - 1)) // r) * r
    ends = jnp.cumsum(padded)
    starts = ends - padded
    pos = starts[eid] + rank[:TOP_K]
    tile_expert = jnp.minimum(
        jnp.searchsorted(ends, jnp.arange(n_tiles, dtype=I32) * r, side="right"), N_EXPERTS - 1).astype(I32)
    n_used = (ends[-1] // r).astype(I32).reshape(1)
    tok = jnp.tile(jnp.arange(T, dtype=I32), TOP_K)
    tok_of_row = jnp.zeros((P,), I32).at[pos.reshape(-1)].set(tok)
    xs = jnp.take(x1.astype(BF16), tok_of_row, axis=0)
    y = _experts(tile_expert + layer * N_EXPERTS, n_used, xs, wg, wu, bg, bu, wd, bd)
    return jnp.take(y, pos, axis=0)


def _rope_table(seqs, T):
    pos = np.zeros((T,), np.int32)
    for s0, n in seqs:
        pos[s0:s0 + n] = np.arange(n)
    pos = jnp.asarray(pos)
    row = (pos // GRID_W).astype(F32)
    col = (pos % GRID_W).astype(F32)
    freqs = ROPE_THETA ** (-jnp.arange(0, AXIS_DIM, 2, dtype=F32) / AXIS_DIM)
    ar = freqs[:, None] * row[None, :]
    ac = freqs[:, None] * col[None, :]
    return jnp.concatenate([jnp.cos(ar), jnp.sin(ar), jnp.cos(ac), jnp.sin(ac)], axis=0)


def _trunk(x, seqs, gqa_w_qkv, gqa_q_norm, gqa_k_norm, gqa_w_o, na_w_qkv, na_rpb, na_w_o,
           ln1_g, ln1_b, ln2_g, ln2_b, router_w, router_b, exp_w_gu, exp_b_gu, exp_w_down, exp_b_down):
    T = x.shape[0]
    tab = _rope_table(seqs, T)
    kv_start = np.zeros((T // Q_TILE,), np.int32)
    kv_chunks = np.zeros((T // Q_TILE,), np.int32)
    nq = NA_QROWS * GRID_W
    na_koff = np.zeros((T // nq,), np.int32)
    na_case = np.ones((T // nq,), np.int32)
    for s0, n in seqs:
        assert n % (2 * KV_CHUNK) == 0 and s0 % Q_TILE == 0 and n % Q_TILE == 0, (s0, n)
        kv_start[s0 // Q_TILE:(s0 + n) // Q_TILE] = s0
        kv_chunks[s0 // Q_TILE:(s0 + n) // Q_TILE] = n // KV_CHUNK
        rows = n // GRID_W
        for r0 in range(0, rows, NA_QROWS):
            ws = min(max(r0 - NA_ROWS // 2, 0), rows - NA_KROWS)
            rb = (s0 + r0 * GRID_W) // nq
            na_koff[rb] = s0 + ws * GRID_W
            na_case[rb] = 0 if r0 == 0 else (2 if r0 == rows - NA_QROWS else 1)
    kv_start, kv_chunks = jnp.asarray(kv_start), jnp.asarray(kv_chunks)
    na_koff, na_case = jnp.asarray(na_koff), jnp.asarray(na_case)

    ne = DEPTH * N_EXPERTS
    wg_all, wu_all = _wgu_prep(exp_w_gu.reshape(ne, D_MODEL, 2 * D_FF))
    bgu = exp_b_gu.reshape(ne, 1, 2 * D_FF)
    bg_all, bu_all = bgu[..., 0::2], bgu[..., 1::2]
    wd_all = exp_w_down.reshape(ne, D_FF, D_MODEL).astype(BF16)
    bd_all = exp_b_down.reshape(ne, 1, D_MODEL)

    for i in range(DEPTH):
        j = i // 2
        if i % 2 == 0:
            wT = gqa_w_qkv[j].T.astype(BF16)
            qT, k, vT = _gqa_prep(x, wT, gqa_q_norm[j][:, None], gqa_k_norm[j][:, None], tab)
            oT = _gqa_attn(qT, k, vT, kv_start, kv_chunks)
            wo = gqa_w_o[j].astype(BF16)
        else:
            w = na_w_qkv[j]
            qT, k, vT = _na_prep(x, w[:, :D_MODEL].T.astype(BF16), w[:, D_MODEL:2 * D_MODEL].astype(BF16),
                                 w[:, 2 * D_MODEL:].T.astype(BF16))
            oT = _na_attn(qT, k, vT, _na_bias_tables(na_rpb[j]), na_koff, na_case)
            wo = na_w_o[j].astype(BF16)
        x1, eid, wts, rank, cnt = _proj_ln_router(oT, wo, x, ln1_g[i][None], ln1_b[i][None],
                                                  router_w[i].T, router_b[i][:, None])
        yg = _moe(x1, eid, rank, cnt[:, 0], i, wg_all, wu_all, bg_all, bu_all, wd_all, bd_all)
        x = _combine_ln(yg, wts[:TOP_K].T, x1, ln2_g[i][None], ln2_b[i][None])
    return x


def kernel(x_prompt, x_sample, gqa_w_qkv, gqa_q_norm, gqa_k_norm, gqa_w_o, na_w_qkv, na_rpb, na_w_o, ln1_g, ln1_b, ln2_g, ln2_b, router_w, router_b, exp_w_gu, exp_b_gu, exp_w_down, exp_b_down):
    bp, sp, d = x_prompt.shape
    bs, ss, _ = x_sample.shape
    x = jnp.concatenate([x_prompt.reshape(bp * sp, d), x_sample.reshape(bs * ss, d)], axis=0)
    seqs = tuple((b * sp, sp) for b in range(bp)) + tuple((bp * sp + b * ss, ss) for b in range(bs))
    y = _trunk(x, seqs, gqa_w_qkv, gqa_q_norm, gqa_k_norm, gqa_w_o, na_w_qkv, na_rpb, na_w_o,
               ln1_g, ln1_b, ln2_g, ln2_b, router_w, router_b, exp_w_gu, exp_b_gu, exp_w_down, exp_b_down)
    return (y[:bp * sp].reshape(bp, sp, d), y[bp * sp:].reshape(bs, ss, d))
```

```python
import functools
import math

import numpy as np
import jax
import jax.numpy as jnp
from jax import lax
from jax.experimental import pallas as pl
from jax.experimental.pallas import tpu as pltpu

F32 = jnp.float32
BF16 = jnp.bfloat16
I32 = jnp.int32

D_MODEL = 1024
DEPTH = 4
GRID_W = 64
HEAD_DIM = 64
N_HEADS = 16
N_KV_HEADS = 4
GQA_GROUP = 4
QKV_DIM = 1536
ROPE_THETA = 10000.0
AXIS_DIM = 32
NA_ROWS = 8
NA_COLS = 16
N_EXPERTS = 32
TOP_K = 4
D_FF = 1024
SWIGLU_LIMIT = 7.0
SWIGLU_ALPHA = 1.702
DN_ALPHA = (2 * DEPTH) ** 0.25
LN_EPS = 1e-5
RMS_EPS = 1e-6
LOG2E = 1.4426950408889634
NEG_BIG = -1e30

VMEM_LIMIT = 56 * 1024 * 1024
TOK_TILE = 512
Q_TILE = 256
KV_CHUNK = 256
NA_QROWS = 8
NA_KROWS = 16
EXP_TILE = 512

NT_DIMS = (((1,), (1,)), ((), ()))
TN_DIMS = (((0,), (0,)), ((), ()))


def _cparams(sem):
    return pltpu.CompilerParams(dimension_semantics=sem, vmem_limit_bytes=VMEM_LIMIT)


def _layer_norm(y, g, b):
    mu = jnp.mean(y, axis=-1, keepdims=True)
    yc = y - mu
    var = jnp.mean(yc * yc, axis=-1, keepdims=True)
    return yc * lax.rsqrt(var + LN_EPS) * g + b


def _rope_t(y, tab):
    cr, sr, cc, sc = tab[0:16], tab[16:32], tab[32:48], tab[48:64]
    a, b, c, d = y[0:16], y[16:32], y[32:48], y[48:64]
    return jnp.concatenate([a * cr - b * sr, b * cr + a * sr, c * cc - d * sc, d * cc + c * sc], axis=0)


def _gqa_prep_kernel(x_ref, w_ref, qg_ref, kg_ref, tab_ref, qT_ref, k_ref, vT_ref):
    xb = x_ref[...].astype(BF16)
    accT = lax.dot_general(w_ref[...], xb, NT_DIMS, preferred_element_type=F32)
    tab = tab_ref[...]

    def normrope(blk, g):
        ms = jnp.mean(blk * blk, axis=0, keepdims=True)
        return _rope_t(blk * lax.rsqrt(ms + RMS_EPS) * g, tab)

    qg = qg_ref[...]
    kg = kg_ref[...]
    for h in range(N_HEADS):
        y = normrope(accT[HEAD_DIM * h:HEAD_DIM * (h + 1)], qg) * (HEAD_DIM ** -0.5 * LOG2E)
        qT_ref[HEAD_DIM * h:HEAD_DIM * (h + 1), :] = y.astype(BF16)
    k0 = N_HEADS * HEAD_DIM
    ks = [normrope(accT[k0 + HEAD_DIM * h:k0 + HEAD_DIM * (h + 1)], kg) for h in range(N_KV_HEADS)]
    k_ref[...] = jnp.concatenate(ks, axis=0).T.astype(BF16)
    v0 = k0 + N_KV_HEADS * HEAD_DIM
    vT_ref[...] = accT[v0:v0 + N_KV_HEADS * HEAD_DIM].astype(BF16)


def _gqa_prep(x, wT, qg, kg, tab):
    T = x.shape[0]
    t = TOK_TILE
    kvd = N_KV_HEADS * HEAD_DIM
    return pl.pallas_call(
        _gqa_prep_kernel,
        grid=(T // t,),
        in_specs=[
            pl.BlockSpec((t, D_MODEL), lambda i: (i, 0)),
            pl.BlockSpec((QKV_DIM, D_MODEL), lambda i: (0, 0)),
            pl.BlockSpec((HEAD_DIM, 1), lambda i: (0, 0)),
            pl.BlockSpec((HEAD_DIM, 1), lambda i: (0, 0)),
            pl.BlockSpec((HEAD_DIM, t), lambda i: (0, i)),
        ],
        out_specs=[
            pl.BlockSpec((D_MODEL, t), lambda i: (0, i)),
            pl.BlockSpec((t, kvd), lambda i: (i, 0)),
            pl.BlockSpec((kvd, t), lambda i: (0, i)),
        ],
        out_shape=[
            jax.ShapeDtypeStruct((D_MODEL, T), BF16),
            jax.ShapeDtypeStruct((T, kvd), BF16),
            jax.ShapeDtypeStruct((kvd, T), BF16),
        ],
        compiler_params=_cparams(("parallel",)),
        name="gqa_prep",
    )(x, wT, qg, kg, tab)


def _gqa_attn_kernel(kvs_ref, kvn_ref, q_ref, k_ref, vT_ref, o_ref, s_ref):
    kh = pl.program_id(0)
    qi = pl.program_id(1)
    tq = q_ref.shape[1]
    q4 = q_ref[...]
    par = kh % 2
    zero = jnp.zeros((HEAD_DIM, tq), BF16)
    cols = []
    for g in range(GQA_GROUP):
        qg = q4[HEAD_DIM * g:HEAD_DIM * (g + 1)]
        cols.append(jnp.concatenate([jnp.where(par == 0, qg, zero), jnp.where(par == 1, qg, zero)], axis=0))
    qcat = jnp.concatenate(cols, axis=1)
    M = GQA_GROUP * tq
    start = kvs_ref[qi]
    nchunk = kvn_ref[qi]

    def scores(slot, c):
        off = pl.multiple_of(start + c * KV_CHUNK, KV_CHUNK)
        kc = k_ref[pl.ds(off, KV_CHUNK), :]
        s_ref[slot] = jnp.dot(kc, qcat, preferred_element_type=F32)

    ones = jnp.ones((16, KV_CHUNK), BF16)

    def softmax_pv(slot, c, m, acc):
        sT = s_ref[slot]
        m_new = jnp.maximum(m, jnp.max(sT, axis=0, keepdims=True))
        alpha = jnp.exp2(m - m_new)
        p = jnp.exp2(sT - m_new)
        off = pl.multiple_of(start + c * KV_CHUNK, KV_CHUNK)
        vc = jnp.concatenate([vT_ref[:, pl.ds(off, KV_CHUNK)], ones], axis=0)
        acc = alpha * acc + jnp.dot(vc, p.astype(BF16), preferred_element_type=F32)
        return m_new, acc

    scores(0, 0)

    def body(j, carry):
        m, acc = carry
        c0 = 2 * j
        scores(1, c0 + 1)
        m, acc = softmax_pv(0, c0, m, acc)
        scores(0, jnp.minimum(c0 + 2, nchunk - 1))
        return softmax_pv(1, c0 + 1, m, acc)

    m0 = jnp.full((1, M), -jnp.inf, F32)
    a0 = jnp.zeros((HEAD_DIM + 16, M), F32)
    _, acc = lax.fori_loop(0, nchunk // 2, body, (m0, a0))
    oT = acc[:HEAD_DIM] * (1.0 / acc[HEAD_DIM:HEAD_DIM + 1])
    o_ref[...] = jnp.concatenate([oT[:, g * tq:(g + 1) * tq] for g in range(GQA_GROUP)], axis=0).astype(BF16)


def _gqa_attn(qT, k, vT, kv_start, kv_chunks):
    T = qT.shape[1]
    tq = Q_TILE
    grid_spec = pltpu.PrefetchScalarGridSpec(
        num_scalar_prefetch=2,
        grid=(N_KV_HEADS, T // tq),
        in_specs=[
            pl.BlockSpec((GQA_GROUP * HEAD_DIM, tq), lambda kh, qi, a, b: (kh, qi)),
            pl.BlockSpec((T, 2 * HEAD_DIM), lambda kh, qi, a, b: (0, kh // 2)),
            pl.BlockSpec((HEAD_DIM, T), lambda kh, qi, a, b: (kh, 0)),
        ],
        out_specs=pl.BlockSpec((GQA_GROUP * HEAD_DIM, tq), lambda kh, qi, a, b: (kh, qi)),
        scratch_shapes=[pltpu.VMEM((2, KV_CHUNK, GQA_GROUP * tq), F32)],
    )
    return pl.pallas_call(
        _gqa_attn_kernel,
        grid_spec=grid_spec,
        out_shape=jax.ShapeDtypeStruct((D_MODEL, T), BF16),
        compiler_params=_cparams(("parallel", "parallel")),
        name="gqa_attn",
    )(kv_start, kv_chunks, qT, k, vT)


def _na_prep_kernel(x_ref, wqT_ref, wk_ref, wvT_ref, qT_ref, k_ref, vT_ref):
    xb = x_ref[...].astype(BF16)
    qT = lax.dot_general(wqT_ref[...], xb, NT_DIMS, preferred_element_type=F32)
    qT_ref[...] = (qT * (HEAD_DIM ** -0.5 * LOG2E)).astype(BF16)
    k_ref[...] = jnp.dot(xb, wk_ref[...], preferred_element_type=F32).astype(BF16)
    vT_ref[...] = lax.dot_general(wvT_ref[...], xb, NT_DIMS, preferred_element_type=F32).astype(BF16)


def _na_prep(x, wqT, wk, wvT):
    T = x.shape[0]
    t = TOK_TILE
    wspec = pl.BlockSpec((D_MODEL, D_MODEL), lambda i: (0, 0))
    return pl.pallas_call(
        _na_prep_kernel,
        grid=(T // t,),
        in_specs=[pl.BlockSpec((t, D_MODEL), lambda i: (i, 0)), wspec, wspec, wspec],
        out_specs=[
            pl.BlockSpec((D_MODEL, t), lambda i: (0, i)),
            pl.BlockSpec((t, D_MODEL), lambda i: (i, 0)),
            pl.BlockSpec((D_MODEL, t), lambda i: (0, i)),
        ],
        out_shape=[
            jax.ShapeDtypeStruct((D_MODEL, T), BF16),
            jax.ShapeDtypeStruct((T, D_MODEL), BF16),
            jax.ShapeDtypeStruct((D_MODEL, T), BF16),
        ],
        compiler_params=_cparams(("parallel",)),
        name="na_prep",
    )(x, wqT, wk, wvT)


def _na_attn_kernel(koff_ref, case_ref, q_ref, k_ref, vT_ref, bias_ref, o_ref):
    rb = pl.program_id(1)
    nq = q_ref.shape[1]
    nk = NA_KROWS * GRID_W
    off = pl.multiple_of(koff_ref[rb], 2 * HEAD_DIM)
    kc = k_ref[pl.ds(off, nk), :]
    q2 = q_ref[...]
    zero = jnp.zeros((HEAD_DIM, nq), BF16)
    sTs = []
    for hh in range(2):
        qh = q2[HEAD_DIM * hh:HEAD_DIM * (hh + 1)]
        qpad = jnp.concatenate([qh, zero] if hh == 0 else [zero, qh], axis=0)
        sTs.append(jnp.dot(kc, qpad, preferred_element_type=F32))
    for hh in range(2):
        sT = sTs[hh] + bias_ref[0, hh]
        m = jnp.max(sT, axis=0, keepdims=True)
        p = jnp.exp2(sT - m)
        l = jnp.sum(p, axis=0, keepdims=True)
        vc = vT_ref[HEAD_DIM * hh:HEAD_DIM * (hh + 1), pl.ds(off, nk)]
        oT = jnp.dot(vc, p.astype(BF16), preferred_element_type=F32) * (1.0 / l)
        o_ref[HEAD_DIM * hh:HEAD_DIM * (hh + 1), :] = oT.astype(BF16)


def _na_attn(qT, k, vT, bias, koff, case):
    T = qT.shape[1]
    nq = NA_QROWS * GRID_W
    nk = NA_KROWS * GRID_W
    grid_spec = pltpu.PrefetchScalarGridSpec(
        num_scalar_prefetch=2,
        grid=(N_HEADS // 2, T // nq),
        in_specs=[
            pl.BlockSpec((2 * HEAD_DIM, nq), lambda hp, rb, a, b: (hp, rb)),
            pl.BlockSpec((T, 2 * HEAD_DIM), lambda hp, rb, a, b: (0, hp)),
            pl.BlockSpec((2 * HEAD_DIM, T), lambda hp, rb, a, b: (hp, 0)),
            pl.BlockSpec((1, 2, nk, nq), lambda hp, rb, a, b: (b[rb], hp, 0, 0)),
        ],
        out_specs=pl.BlockSpec((2 * HEAD_DIM, nq), lambda hp, rb, a, b: (hp, rb)),
    )
    return pl.pallas_call(
        _na_attn_kernel,
        grid_spec=grid_spec,
        out_shape=jax.ShapeDtypeStruct((D_MODEL, T), BF16),
        compiler_params=_cparams(("parallel", "parallel")),
        name="na_attn",
    )(koff, case, qT, k, vT, bias)


def _na_bias_tables(rpb):
    R = 64
    a = np.arange(NA_QROWS)
    b = np.arange(NA_KROWS)
    row_oh = np.zeros((3, 2 * NA_ROWS - 1, NA_KROWS, NA_QROWS), np.float32)
    row_ok = np.zeros((3, NA_KROWS, NA_QROWS), bool)
    for c, r0 in enumerate((0, 24, R - NA_QROWS)):
        ws = min(max(r0 - NA_ROWS // 2, 0), R - NA_KROWS)
        r = r0 + a
        rs = np.clip(r - NA_ROWS // 2, 0, R - NA_ROWS)
        krow = ws + b
        rel = krow[:, None] - rs[None, :]
        ok = (rel >= 0) & (rel < NA_ROWS)
        dr = krow[:, None] - r[None, :] + (NA_ROWS - 1)
        row_ok[c] = ok
        for bi in range(NA_KROWS):
            for ai in range(NA_QROWS):
                if ok[bi, ai]:
                    row_oh[c, dr[bi, ai], bi, ai] = 1.0
    w = np.arange(GRID_W)
    cs = np.clip(w - NA_COLS // 2, 0, GRID_W - NA_COLS)
    kc = np.arange(GRID_W)
    relc = kc[:, None] - cs[None, :]
    col_ok = (relc >= 0) & (relc < NA_COLS)
    dc = kc[:, None] - w[None, :] + (NA_COLS - 1)
    col_oh = np.zeros((2 * NA_COLS - 1, GRID_W, GRID_W), np.float32)
    for ki in range(GRID_W):
        for wi in range(GRID_W):
            if col_ok[ki, wi]:
                col_oh[dc[ki, wi], ki, wi] = 1.0
    ok = row_ok[:, :, None, :, None] & col_ok[None, None, :, None, :]
    t = jnp.einsum("hrc,xrba,ckw->xhbkaw", rpb * LOG2E, jnp.asarray(row_oh), jnp.asarray(col_oh),
                   precision=lax.Precision.HIGHEST)
    t = jnp.where(jnp.asarray(ok)[:, None], t, NEG_BIG)
    return t.reshape(3, N_HEADS, NA_KROWS * GRID_W, NA_QROWS * GRID_W)


def _proj_ln_router_kernel(oT_ref, wo_ref, x_ref, g_ref, b_ref, wrT_ref, br_ref,
                           x1_ref, eid_ref, wts_ref, rank_ref, cnt_ref, carry_ref):
    i = pl.program_id(0)
    t = x_ref.shape[0]

    @pl.when(i == 0)
    def _():
        carry_ref[...] = jnp.zeros_like(carry_ref)

    h = lax.dot_general(oT_ref[...], wo_ref[...], TN_DIMS, preferred_element_type=F32)
    x1 = _layer_norm(DN_ALPHA * x_ref[...] + h, g_ref[...], b_ref[...])
    x1_ref[...] = x1

    xh = x1.astype(BF16)
    xl = (x1 - xh.astype(F32)).astype(BF16)
    wr = wrT_ref[...]
    wh = wr.astype(BF16)
    wl = (wr - wh.astype(F32)).astype(BF16)
    logits = (lax.dot_general(wh, xh, NT_DIMS, preferred_element_type=F32)
              + lax.dot_general(wh, xl, NT_DIMS, preferred_element_type=F32)
              + lax.dot_general(wl, xh, NT_DIMS, preferred_element_type=F32)) + br_ref[...]

    eidx = lax.broadcasted_iota(I32, (N_EXPERTS, t), 0).astype(F32)
    work = logits
    vals, ids, ohs = [], [], []
    for _ in range(TOP_K):
        mx = jnp.max(work, axis=0, keepdims=True)
        idx = jnp.min(jnp.where(work == mx, eidx, float(N_EXPERTS)), axis=0, keepdims=True)
        oh = eidx == idx
        vals.append(mx)
        ids.append(idx)
        ohs.append(oh)
        work = jnp.where(oh, -jnp.inf, work)
    es = [jnp.exp(v - vals[0]) for v in vals]
    inv = 1.0 / (es[0] + es[1] + es[2] + es[3])
    zrow = jnp.zeros((8 - TOP_K, t), F32)
    wts_ref[...] = jnp.concatenate([e * inv for e in es] + [zrow], axis=0)
    eid_ref[...] = jnp.concatenate(ids + [zrow], axis=0).astype(I32)

    oh_all = jnp.zeros((N_EXPERTS, t), F32)
    for oh in ohs:
        oh_all = oh_all + oh.astype(F32)
    tri = (lax.broadcasted_iota(I32, (t, t), 0) < lax.broadcasted_iota(I32, (t, t), 1)).astype(BF16)
    base = carry_ref[...] + jnp.dot(oh_all.astype(BF16), tri, preferred_element_type=F32)
    ranks = [jnp.sum(jnp.where(oh, base, 0.0), axis=0, keepdims=True) for oh in ohs]
    rank_ref[...] = jnp.concatenate(ranks + [zrow], axis=0).astype(I32)
    carry = carry_ref[...] + jnp.sum(oh_all, axis=1, keepdims=True)
    carry_ref[...] = carry
    cnt_ref[...] = jnp.broadcast_to(carry, cnt_ref.shape).astype(I32)


def _proj_ln_router(oT, wo, x, g, b, wrT, br):
    T = x.shape[0]
    t = TOK_TILE
    row = lambda i: (i, 0)
    col = lambda i: (0, i)
    fixed = lambda i: (0, 0)
    return pl.pallas_call(
        _proj_ln_router_kernel,
        grid=(T // t,),
        in_specs=[
            pl.BlockSpec((D_MODEL, t), col),
            pl.BlockSpec((D_MODEL, D_MODEL), fixed),
            pl.BlockSpec((t, D_MODEL), row),
            pl.BlockSpec((1, D_MODEL), fixed),
            pl.BlockSpec((1, D_MODEL), fixed),
            pl.BlockSpec((N_EXPERTS, D_MODEL), fixed),
            pl.BlockSpec((N_EXPERTS, 1), fixed),
        ],
        out_specs=[
            pl.BlockSpec((t, D_MODEL), row),
            pl.BlockSpec((8, t), col),
            pl.BlockSpec((8, t), col),
            pl.BlockSpec((8, t), col),
            pl.BlockSpec((N_EXPERTS, 128), fixed),
        ],
        out_shape=[
            jax.ShapeDtypeStruct((T, D_MODEL), F32),
            jax.ShapeDtypeStruct((8, T), I32),
            jax.ShapeDtypeStruct((8, T), F32),
            jax.ShapeDtypeStruct((8, T), I32),
            jax.ShapeDtypeStruct((N_EXPERTS, 128), I32),
        ],
        scratch_shapes=[pltpu.VMEM((N_EXPERTS, 1), F32)],
        compiler_params=_cparams(("arbitrary",)),
        name="proj_ln_router",
    )(oT, wo, x, g, b, wrT, br)


def _deinterleave_perm():
    p = np.zeros((256, 256), np.float32)
    k = np.arange(128)
    p[2 * k, k] = 1.0
    p[2 * k + 1, 128 + k] = 1.0
    return jnp.asarray(p, BF16)


def _wgu_prep_kernel(w_ref, perm_ref, wg_ref, wu_ref):
    w = w_ref[0].astype(BF16)
    perm = perm_ref[...]
    for c in range(2 * D_FF // 256):
        r = jnp.dot(w[:, 256 * c:256 * (c + 1)], perm, preferred_element_type=F32)
        wg_ref[0, :, 128 * c:128 * (c + 1)] = r[:, :128].astype(BF16)
        wu_ref[0, :, 128 * c:128 * (c + 1)] = r[:, 128:].astype(BF16)


def _wgu_prep(w_gu):
    n = w_gu.shape[0]
    rt = 512
    return pl.pallas_call(
        _wgu_prep_kernel,
        grid=(n, D_MODEL // rt),
        in_specs=[
            pl.BlockSpec((1, rt, 2 * D_FF), lambda e, i: (e, i, 0)),
            pl.BlockSpec((256, 256), lambda e, i: (0, 0)),
        ],
        out_specs=[
            pl.BlockSpec((1, rt, D_FF), lambda e, i: (e, i, 0)),
            pl.BlockSpec((1, rt, D_FF), lambda e, i: (e, i, 0)),
        ],
        out_shape=[
            jax.ShapeDtypeStruct((n, D_MODEL, D_FF), BF16),
            jax.ShapeDtypeStruct((n, D_MODEL, D_FF), BF16),
        ],
        compiler_params=_cparams(("parallel", "parallel")),
        name="wgu_prep",
    )(w_gu, _deinterleave_perm())


def _experts_kernel(te_ref, nused_ref, xs_ref, wg_ref, wu_ref, bg_ref, bu_ref, wd_ref, bd_ref, y_ref):
    i = pl.program_id(0)

    @pl.when(i < nused_ref[0])
    def _():
        x = xs_ref[...]
        g = jnp.dot(x, wg_ref[0], preferred_element_type=F32) + bg_ref[0]
        u = jnp.dot(x, wu_ref[0], preferred_element_type=F32) + bu_ref[0]
        g = jnp.minimum(g, SWIGLU_LIMIT)
        u = jnp.clip(u, -SWIGLU_LIMIT, SWIGLU_LIMIT)
        a = (u + 1.0) * (g * jax.nn.sigmoid(SWIGLU_ALPHA * g))
        y = jnp.dot(a.astype(BF16), wd_ref[0], preferred_element_type=F32) + bd_ref[0]
        y_ref[...] = y.astype(BF16)

    @pl.when(i >= nused_ref[0])
    def _():
        y_ref[...] = jnp.zeros_like(y_ref)


def _experts(tile_expert, n_used, xs, wg, wu, bg, bu, wd, bd):
    P = xs.shape[0]
    r = EXP_TILE
    wmap = lambda i, te, nu: (te[i], 0, 0)
    grid_spec = pltpu.PrefetchScalarGridSpec(
        num_scalar_prefetch=2,
        grid=(P // r,),
        in_specs=[
            pl.BlockSpec((r, D_MODEL), lambda i, te, nu: (i, 0)),
            pl.BlockSpec((1, D_MODEL, D_FF), wmap),
            pl.BlockSpec((1, D_MODEL, D_FF), wmap),
            pl.BlockSpec((1, 1, D_FF), wmap),
            pl.BlockSpec((1, 1, D_FF), wmap),
            pl.BlockSpec((1, D_FF, D_MODEL), wmap),
            pl.BlockSpec((1, 1, D_MODEL), wmap),
        ],
        out_specs=pl.BlockSpec((r, D_MODEL), lambda i, te, nu: (i, 0)),
    )
    return pl.pallas_call(
        _experts_kernel,
        grid_spec=grid_spec,
        out_shape=jax.ShapeDtypeStruct((P, D_MODEL), BF16),
        compiler_params=_cparams(("parallel",)),
        name="experts",
    )(tile_expert, n_used, xs, wg, wu, bg, bu, wd, bd)


def _combine_ln_kernel(yg_ref, wt_ref, x_ref, g_ref, b_ref, o_ref):
    wt = wt_ref[...]
    f = yg_ref[0].astype(F32) * wt[:, 0:1]
    for r in range(1, TOP_K):
        f = f + yg_ref[r].astype(F32) * wt[:, r:r + 1]
    o_ref[...] = _layer_norm(DN_ALPHA * x_ref[...] + f, g_ref[...], b_ref[...])


def _combine_ln(yg, wt, x, g, b):
    T = x.shape[0]
    t = TOK_TILE
    return pl.pallas_call(
        _combine_ln_kernel,
        grid=(T // t,),
        in_specs=[
            pl.BlockSpec((TOP_K, t, D_MODEL), lambda i: (0, i, 0)),
            pl.BlockSpec((t, TOP_K), lambda i: (i, 0)),
            pl.BlockSpec((t, D_MODEL), lambda i: (i, 0)),
            pl.BlockSpec((1, D_MODEL), lambda i: (0, 0)),
            pl.BlockSpec((1, D_MODEL), lambda i: (0, 0)),
        ],
        out_specs=pl.BlockSpec((t, D_MODEL), lambda i: (i, 0)),
        out_shape=jax.ShapeDtypeStruct((T, D_MODEL), F32),
        compiler_params=_cparams(("parallel",)),
        name="combine_ln",
    )(yg, wt, x, g, b)


def _moe(x1, eid, rank, counts, layer, wg, wu, bg, bu, wd, bd):
    T = x1.shape[0]
    r = EXP_TILE
    P = T * TOP_K + N_EXPERTS * r
    n_tiles = P // r
    eid = eid[:TOP_K]
    padded = ((counts + (r - 1)) // r) * r
    ends = jnp.cumsum(padded)
    starts = ends - padded
    pos = starts[eid] + rank[:TOP_K]
    tile_expert = jnp.minimum(
        jnp.searchsorted(ends, jnp.arange(n_tiles, dtype=I32) * r, side="right"), N_EXPERTS - 1).astype(I32)
    n_used = (ends[-1] // r).astype(I32).reshape(1)
    tok = jnp.tile(jnp.arange(T, dtype=I32), TOP_K)
    tok_of_row = jnp.zeros((P,), I32).at[pos.reshape(-1)].set(tok)
    xs = jnp.take(x1.astype(BF16), tok_of_row, axis=0)
    y = _experts(tile_expert + layer * N_EXPERTS, n_used, xs, wg, wu, bg, bu, wd, bd)
    return jnp.take(y, pos, axis=0)


def _rope_table(seqs, T):
    pos = np.zeros((T,), np.int32)
    for s0, n in seqs:
        pos[s0:s0 + n] = np.arange(n)
    pos = jnp.asarray(pos)
    row = (pos // GRID_W).astype(F32)
    col = (pos % GRID_W).astype(F32)
    freqs = ROPE_THETA ** (-jnp.arange(0, AXIS_DIM, 2, dtype=F32) / AXIS_DIM)
    ar = freqs[:, None] * row[None, :]
    ac = freqs[:, None] * col[None, :]
    return jnp.concatenate([jnp.cos(ar), jnp.sin(ar), jnp.cos(ac), jnp.sin(ac)], axis=0)


def _trunk(x, seqs, gqa_w_qkv, gqa_q_norm, gqa_k_norm, gqa_w_o, na_w_qkv, na_rpb, na_w_o,
           ln1_g, ln1_b, ln2_g, ln2_b, router_w, router_b, exp_w_gu, exp_b_gu, exp_w_down, exp_b_down):
    T = x.shape[0]
    tab = _rope_table(seqs, T)
    kv_start = np.zeros((T // Q_TILE,), np.int32)
    kv_chunks = np.zeros((T // Q_TILE,), np.int32)
    nq = NA_QROWS * GRID_W
    na_koff = np.zeros((T // nq,), np.int32)
    na_case = np.ones((T // nq,), np.int32)
    for s0, n in seqs:
        assert n % (2 * KV_CHUNK) == 0 and s0 % Q_TILE == 0 and n % Q_TILE == 0, (s0, n)
        kv_start[s0 // Q_TILE:(s0 + n) // Q_TILE] = s0
        kv_chunks[s0 // Q_TILE:(s0 + n) // Q_TILE] = n // KV_CHUNK
        rows = n // GRID_W
        for r0 in range(0, rows, NA_QROWS):
            ws = min(max(r0 - NA_ROWS // 2, 0), rows - NA_KROWS)
            rb = (s0 + r0 * GRID_W) // nq
            na_koff[rb] = s0 + ws * GRID_W
            na_case[rb] = 0 if r0 == 0 else (2 if r0 == rows - NA_QROWS else 1)
    kv_start, kv_chunks = jnp.asarray(kv_start), jnp.asarray(kv_chunks)
    na_koff, na_case = jnp.asarray(na_koff), jnp.asarray(na_case)

    ne = DEPTH * N_EXPERTS
    wg_all, wu_all = _wgu_prep(exp_w_gu.reshape(ne, D_MODEL, 2 * D_FF))
    bgu = exp_b_gu.reshape(ne, 1, 2 * D_FF)
    bg_all, bu_all = bgu[..., 0::2], bgu[..., 1::2]
    wd_all = exp_w_down.reshape(ne, D_FF, D_MODEL).astype(BF16)
    bd_all = exp_b_down.reshape(ne, 1, D_MODEL)

    for i in range(DEPTH):
        j = i // 2
        if i % 2 == 0:
            wT = gqa_w_qkv[j].T.astype(BF16)
            qT, k, vT = _gqa_prep(x, wT, gqa_q_norm[j][:, None], gqa_k_norm[j][:, None], tab)
            oT = _gqa_attn(qT, k, vT, kv_start, kv_chunks)
            wo = gqa_w_o[j].astype(BF16)
        else:
            w = na_w_qkv[j]
            qT, k, vT = _na_prep(x, w[:, :D_MODEL].T.astype(BF16), w[:, D_MODEL:2 * D_MODEL].astype(BF16),
                                 w[:, 2 * D_MODEL:].T.astype(BF16))
            oT = _na_attn(qT, k, vT, _na_bias_tables(na_rpb[j]), na_koff, na_case)
            wo = na_w_o[j].astype(BF16)
        x1, eid, wts, rank, cnt = _proj_ln_router(oT, wo, x, ln1_g[i][None], ln1_b[i][None],
                                                  router_w[i].T, router_b[i][:, None])
        yg = _moe(x1, eid, rank, cnt[:, 0], i, wg_all, wu_all, bg_all, bu_all, wd_all, bd_all)
        x = _combine_ln(yg, wts[:TOP_K].T, x1, ln2_g[i][None], ln2_b[i][None])
    return x


def kernel(x_prompt, x_sample, gqa_w_qkv, gqa_q_norm, gqa_k_norm, gqa_w_o, na_w_qkv, na_rpb, na_w_o, ln1_g, ln1_b, ln2_g, ln2_b, router_w, router_b, exp_w_gu, exp_b_gu, exp_w_down, exp_b_down):
    bp, sp, d = x_prompt.shape
    bs, ss, _ = x_sample.shape
    x = jnp.concatenate([x_prompt.reshape(bp * sp, d), x_sample.reshape(bs * ss, d)], axis=0)
    seqs = tuple((b * sp, sp) for b in range(bp)) + tuple((bp * sp + b * ss, ss) for b in range(bs))
    y = _trunk(x, seqs, gqa_w_qkv, gqa_q_norm, gqa_k_norm, gqa_w_o, na_w_qkv, na_rpb, na_w_o,
               ln1_g, ln1_b, ln2_g, ln2_b, router_w, router_b, exp_w_gu, exp_b_gu, exp_w_down, exp_b_down)
    return (y[:bp * sp].reshape(bp, sp, d), y[bp * sp:].reshape(bs, ss, d))
```
